```python
import math
import functools
import jax
import jax.numpy as jnp
from jax import lax
import numpy as np

D_MODEL = 1024
BATCH = 2
SEQ = 8192
DEPTH = 2
DEC_BATCH = 32
DEC_SEQ = 4
PAST_LEN = 16384
PAGE_SIZE = 128

MIX_W = D_MODEL // 2
N_DIFF_HEADS = 4
DIFF_V_DIM = MIX_W // N_DIFF_HEADS
DIFF_QK_DIM = DIFF_V_DIM // 2
N_DELTA_HEADS = 4
DELTA_DIM = MIX_W // N_DELTA_HEADS
CONV_W = 4
DELTA_CHUNK = 64
Q_BLOCK = 128
N_BUCKETS = 32
MAX_DISTANCE = 128
ALPHA = (2 * DEPTH) ** 0.25
BETA_INIT = (8 * DEPTH) ** -0.25
LN_EPS = 1e-5
NORM_EPS = 1e-6
NEG_INF = -1e30
PROJ_WIDTHS = (MIX_W, MIX_W, MIX_W, MIX_W, MIX_W, MIX_W, MIX_W, MIX_W, N_DELTA_HEADS, N_DELTA_HEADS, 2 * D_MODEL)
PROJ_TOTAL = sum(PROJ_WIDTHS)

kernel_name = 'diffattn_gated_deltanet_gated_merge_deepnorm_step'

F32 = jnp.float32


def split_cols(h):
    idx, acc = [], 0
    for w in PROJ_WIDTHS[:-1]:
        acc += w
        idx.append(acc)
    return jnp.split(h, idx, axis=-1)


def rel_bucket(dist):
    n = jnp.maximum(dist, 0)
    max_exact = N_BUCKETS // 2
    nf = jnp.maximum(n, 1).astype(F32)
    large = max_exact + (jnp.log(nf / max_exact) / math.log(MAX_DISTANCE / max_exact)
                         * (N_BUCKETS - max_exact)).astype(jnp.int32)
    large = jnp.minimum(large, N_BUCKETS - 1)
    return jnp.where(n < max_exact, n, large)


def rel_bias(dist, table):
    return jnp.moveaxis(table[rel_bucket(dist)], -1, 0).astype(F32)


def layer_norm(x, g, b):
    xf = x.astype(F32)
    mu = jnp.mean(xf, axis=-1, keepdims=True)
    var = jnp.mean(jnp.square(xf - mu), axis=-1, keepdims=True)
    return ((xf - mu) * lax.rsqrt(var + LN_EPS) * g.astype(F32) + b.astype(F32)).astype(x.dtype)


def head_rms_norm(x, gain):
    xf = x.astype(F32)
    return xf * lax.rsqrt(jnp.mean(xf * xf, axis=-1, keepdims=True) + NORM_EPS) * gain.astype(F32)


def l2_normalize(x):
    xf = x.astype(F32)
    return xf * lax.rsqrt(jnp.sum(xf * xf, axis=-1, keepdims=True) + NORM_EPS)


def short_conv(x_pre, buf, w):
    T = x_pre.shape[1]
    xpad = jnp.concatenate([buf.astype(x_pre.dtype), x_pre], axis=1)
    y = xpad[:, 0:T] * w[0]
    for j in range(1, CONV_W):
        y = y + xpad[:, j:j + T] * w[j]
    return jax.nn.silu(y), xpad[:, -(CONV_W - 1):]


def gated_delta(q, k, v, beta, g, s0):
    B, T, H, _ = q.shape
    c = math.gcd(T, DELTA_CHUNK)
    n = T // c

    def blk(a):
        return jnp.moveaxis(a.astype(F32).reshape((B, n, c) + a.shape[2:]), 3, 2)

    qc, kc, vc, bc, gc = blk(q), blk(k), blk(v), blk(beta), blk(g)
    gcum = jnp.cumsum(gc, axis=-1)
    diff = gcum[..., :, None] - gcum[..., None, :]
    incl = jnp.tril(jnp.ones((c, c), bool))
    strict = jnp.tril(jnp.ones((c, c), bool), -1)
    decay = jnp.where(incl, jnp.exp(jnp.where(incl, diff, 0.0)), 0.0)
    kb = kc * bc[..., None]
    lmat = jnp.where(strict, jnp.einsum('bnhid,bnhjd->bnhij', kb, kc) * decay, 0.0)
    a_mat = lmat + jnp.eye(c, dtype=F32)
    u = lax.linalg.triangular_solve(a_mat, vc * bc[..., None], left_side=True, lower=True)
    w = lax.linalg.triangular_solve(a_mat, kb * jnp.exp(gcum)[..., None], left_side=True, lower=True)
    a_qk = jnp.where(incl, jnp.einsum('bnhid,bnhjd->bnhij', qc, kc) * decay, 0.0)
    q_dec = qc * jnp.exp(gcum)[..., None]
    g_last = gcum[..., -1]
    k_tail = kc * jnp.exp(g_last[..., None] - gcum)[..., None]

    def step(S, xs):
        u_c, w_c, q_c, a_c, k_c, gl_c = xs
        v_new = u_c - jnp.einsum('bhcd,bhde->bhce', w_c, S)
        o = jnp.einsum('bhcd,bhde->bhce', q_c, S) + jnp.einsum('bhij,bhje->bhie', a_c, v_new)
        S = S * jnp.exp(gl_c)[..., None, None] + jnp.einsum('bhcd,bhce->bhde', k_c, v_new)
        return S, o

    xs = tuple(jnp.moveaxis(a, 1, 0) for a in (u, w, q_dec, a_qk, k_tail, g_last))
    s_fin, o = lax.scan(step, s0.astype(F32), xs)
    o = jnp.moveaxis(jnp.moveaxis(o, 0, 1), 2, 3).reshape(B, T, H, -1)
    return o, s_fin


def diff_attend_prompt(q, k, v, lam, table):
    B, T = q.shape[:2]
    nb = T // Q_BLOCK
    q_blocks = jnp.moveaxis(q.reshape(B, nb, Q_BLOCK, N_DIFF_HEADS, 2, DIFF_QK_DIM), 1, 0)
    kpos = jnp.arange(T)
    scale = DIFF_QK_DIM ** -0.5

    def block(args):
        q_blk, i = args
        qpos = i * Q_BLOCK + jnp.arange(Q_BLOCK)
        dist = qpos[:, None] - kpos[None, :]
        logits = jnp.einsum('bqhmd,bkhmd->bhmqk', q_blk, k).astype(F32) * scale
        logits = logits + rel_bias(dist, table)[None, :, None]
        logits = jnp.where(dist >= 0, logits, NEG_INF)
        p = jax.nn.softmax(logits, axis=-1)
        p_diff = p[:, :, 0] - lam * p[:, :, 1]
        return jnp.einsum('bhqk,bkhv->bqhv', p_diff.astype(v.dtype), v)

    out = lax.map(block, (q_blocks, jnp.arange(nb)))
    return jnp.moveaxis(out, 0, 1).reshape(B, T, N_DIFF_HEADS, DIFF_V_DIM)


def diff_attend_sample(q, k, v, lam, table, cache_k, cache_v, page_table, layer):
    Tq = q.shape[1]
    n_pages = page_table.shape[1]
    past = n_pages * PAGE_SIZE
    qpos = past + jnp.arange(Tq)
    kpos = jnp.concatenate([jnp.arange(past), qpos])
    dist = qpos[:, None] - kpos[None, :]
    bias = rel_bias(dist, table)[:, None]
    causal = dist >= 0
    scale = DIFF_QK_DIM ** -0.5

    def one(args):
        q_b, k_b, v_b, pages = args
        k_past = cache_k[layer, pages].reshape(past, N_DIFF_HEADS, 2, DIFF_QK_DIM)
        v_past = cache_v[layer, pages].reshape(past, N_DIFF_HEADS, DIFF_V_DIM)
        k_all = jnp.concatenate([k_past, k_b.astype(k_past.dtype)], axis=0)
        v_all = jnp.concatenate([v_past, v_b.astype(v_past.dtype)], axis=0)
        logits = jnp.einsum('qhmd,khmd->hmqk', q_b, k_all).astype(F32) * scale + bias
        logits = jnp.where(causal, logits, NEG_INF)
        p = jax.nn.softmax(logits, axis=-1)
        p_diff = p[:, 0] - lam * p[:, 1]
        return jnp.einsum('hqk,khv->qhv', p_diff.astype(v_all.dtype), v_all)

    return lax.map(one, (q, k, v, page_table))


def mixer_sublayer(x, layer, attend, conv_buf, s0, w_in, conv_w, lambda_q1, lambda_k1, lambda_q2,
                   lambda_k2, diff_norm_g, a_log, dt_bias, delta_norm_g, w_branch, w_o):
    B, T, _ = x.shape
    h = jnp.einsum('btd,dc->btc', x, w_in[layer])
    a_q, a_k, a_v, a_g, d_q, d_k, d_v, d_g, d_b, d_a, m_g = split_cols(h)

    lam_init = 0.8 - 0.6 * math.exp(-0.3 * layer)
    lam = (jnp.exp(jnp.sum(lambda_q1[layer].astype(F32) * lambda_k1[layer].astype(F32)))
           - jnp.exp(jnp.sum(lambda_q2[layer].astype(F32) * lambda_k2[layer].astype(F32))) + lam_init)
    q = a_q.reshape(B, T, N_DIFF_HEADS, 2, DIFF_QK_DIM)
    k = a_k.reshape(B, T, N_DIFF_HEADS, 2, DIFF_QK_DIM)
    v = a_v.reshape(B, T, N_DIFF_HEADS, DIFF_V_DIM)
    o_a = attend(q, k, v, lam)
    o_a = head_rms_norm(o_a, diff_norm_g[layer]) * (1.0 - lam_init)
    o_a = (o_a.reshape(B, T, MIX_W) * jax.nn.silu(a_g.astype(F32))).astype(x.dtype)

    qkv_pre = jnp.concatenate([d_q, d_k, d_v], axis=-1)
    qkv, conv_new = short_conv(qkv_pre, conv_buf, conv_w[layer])
    dq, dk, dv = jnp.split(qkv, 3, axis=-1)
    dq = l2_normalize(dq.reshape(B, T, N_DELTA_HEADS, DELTA_DIM)) * DELTA_DIM ** -0.5
    dk = l2_normalize(dk.reshape(B, T, N_DELTA_HEADS, DELTA_DIM))
    dv = dv.reshape(B, T, N_DELTA_HEADS, DELTA_DIM).astype(F32)
    beta = jax.nn.sigmoid(d_b.astype(F32))
    g = -jnp.exp(a_log[layer].astype(F32)) * jax.nn.softplus(d_a.astype(F32) + dt_bias[layer].astype(F32))
    o_d, s_new = gated_delta(dq, dk, dv, beta, g, s0)
    o_d = head_rms_norm(o_d, delta_norm_g[layer]).reshape(B, T, MIX_W)
    o_d = (o_d * jax.nn.silu(d_g.astype(F32))).astype(x.dtype)

    gates = jax.nn.sigmoid(m_g.astype(F32))
    gate_a, gate_d = gates[..., :D_MODEL], gates[..., D_MODEL:]
    h_a = jnp.einsum('btc,cd->btd', o_a, w_branch[layer, 0]).astype(F32)
    h_d = jnp.einsum('btc,cd->btd', o_d, w_branch[layer, 1]).astype(F32)
    merged = (gate_a * h_a + gate_d * h_d).astype(x.dtype)
    y = jnp.einsum('btd,de->bte', merged, w_o[layer])
    k_rows = a_k.reshape(B, T, N_DIFF_HEADS, 2 * DIFF_QK_DIM)
    return y, k_rows, v, s_new.astype(x.dtype), conv_new


def setup_inputs(seed: int = 0) -> dict:
    key = jax.random.key(seed)
    ks = jax.random.split(key, 24)
    n_pages = PAST_LEN // PAGE_SIZE
    n_used = DEC_BATCH * n_pages
    n_pool = n_used + n_used // 4
    page_table = jax.random.permutation(ks[0], n_pool)[:n_used].reshape(DEC_BATCH, n_pages).astype(jnp.int32)
    x_prompt = jax.random.normal(ks[1], (BATCH, SEQ, D_MODEL), F32)
    x_sample = jax.random.normal(ks[2], (DEC_BATCH, DEC_SEQ, D_MODEL), F32)
    cache_k = jax.random.normal(ks[3], (DEPTH, n_pool, PAGE_SIZE, N_DIFF_HEADS, 2 * DIFF_QK_DIM), F32)
    cache_v = jax.random.normal(ks[4], (DEPTH, n_pool, PAGE_SIZE, N_DIFF_HEADS, DIFF_V_DIM), F32)
    state_delta = 0.1 * jax.random.normal(ks[5], (DEPTH, DEC_BATCH, N_DELTA_HEADS, DELTA_DIM, DELTA_DIM), F32)
    state_conv = jax.random.normal(ks[6], (DEPTH, DEC_BATCH, CONV_W - 1, 3 * MIX_W), F32)
    rel_bias_table = 0.5 * jax.random.normal(ks[7], (N_BUCKETS, N_DIFF_HEADS), F32)
    w_in = jax.random.normal(ks[8], (DEPTH, D_MODEL, PROJ_TOTAL), F32) * D_MODEL ** -0.5
    conv_w = 0.5 * jax.random.normal(ks[9], (DEPTH, CONV_W, 3 * MIX_W), F32)
    lambda_q1 = 0.1 * jax.random.normal(ks[10], (DEPTH, DIFF_QK_DIM), F32)
    lambda_k1 = 0.1 * jax.random.normal(ks[11], (DEPTH, DIFF_QK_DIM), F32)
    lambda_q2 = 0.1 * jax.random.normal(ks[12], (DEPTH, DIFF_QK_DIM), F32)
    lambda_k2 = 0.1 * jax.random.normal(ks[13], (DEPTH, DIFF_QK_DIM), F32)
    diff_norm_g = 1.0 + 0.02 * jax.random.normal(ks[14], (DEPTH, DIFF_V_DIM), F32)
    a_log = jnp.log(jax.random.uniform(ks[15], (DEPTH, N_DELTA_HEADS), F32, 1.0, 16.0))
    dt = jnp.exp(jax.random.uniform(ks[16], (DEPTH, N_DELTA_HEADS), F32, math.log(1e-3), math.log(1e-1)))
    dt_bias = dt + jnp.log(-jnp.expm1(-dt))
    delta_norm_g = 1.0 + 0.02 * jax.random.normal(ks[17], (DEPTH, DELTA_DIM), F32)
    w_branch = jax.random.normal(ks[18], (DEPTH, 2, MIX_W, D_MODEL), F32) * (MIX_W ** -0.5 * BETA_INIT)
    w_o = jax.random.normal(ks[19], (DEPTH, D_MODEL, D_MODEL), F32) * (D_MODEL ** -0.5 * BETA_INIT)
    ln_g = 1.0 + 0.02 * jax.random.normal(ks[20], (DEPTH, D_MODEL), F32)
    ln_b = 0.02 * jax.random.normal(ks[21], (DEPTH, D_MODEL), F32)
    return {'x_prompt': x_prompt, 'x_sample': x_sample, 'cache_k': cache_k, 'cache_v': cache_v,
            'state_delta': state_delta, 'state_conv': state_conv, 'page_table': page_table,
            'rel_bias_table': rel_bias_table, 'w_in': w_in, 'conv_w': conv_w,
            'lambda_q1': lambda_q1, 'lambda_k1': lambda_k1, 'lambda_q2': lambda_q2, 'lambda_k2': lambda_k2,
            'diff_norm_g': diff_norm_g, 'a_log': a_log, 'dt_bias': dt_bias, 'delta_norm_g': delta_norm_g,
            'w_branch': w_branch, 'w_o': w_o, 'ln_g': ln_g, 'ln_b': ln_b}


def reference(x_prompt, x_sample, cache_k, cache_v, state_delta, state_conv, page_table, rel_bias_table,
              w_in, conv_w, lambda_q1, lambda_k1, lambda_q2, lambda_k2, diff_norm_g, a_log, dt_bias,
              delta_norm_g, w_branch, w_o, ln_g, ln_b):
    def sub(x, l, attend, conv_buf, s0):
        return mixer_sublayer(x, l, attend, conv_buf, s0, w_in, conv_w, lambda_q1, lambda_k1, lambda_q2,
                              lambda_k2, diff_norm_g, a_log, dt_bias, delta_norm_g, w_branch, w_o)

    attend_prompt = functools.partial(diff_attend_prompt, table=rel_bias_table)
    hp, hs = x_prompt, x_sample
    B = hp.shape[0]
    kp, vp, ks_, vs_, dp, ds, cp, cs = [], [], [], [], [], [], [], []
    for l in range(DEPTH):
        conv0 = jnp.zeros((B, CONV_W - 1, 3 * MIX_W), hp.dtype)
        s0 = jnp.zeros((B, N_DELTA_HEADS, DELTA_DIM, DELTA_DIM), F32)
        y, k_rows, v_rows, s_new, c_new = sub(hp, l, attend_prompt, conv0, s0)
        hp = layer_norm(ALPHA * hp + y, ln_g[l], ln_b[l])
        kp.append(k_rows); vp.append(v_rows); dp.append(s_new); cp.append(c_new)
        attend_sample = functools.partial(diff_attend_sample, table=rel_bias_table, cache_k=cache_k,
                                          cache_v=cache_v, page_table=page_table, layer=l)
        y, k_rows, v_rows, s_new, c_new = sub(hs, l, attend_sample, state_conv[l], state_delta[l])
        hs = layer_norm(ALPHA * hs + y, ln_g[l], ln_b[l])
        ks_.append(k_rows); vs_.append(v_rows); ds.append(s_new); cs.append(c_new)
    return (hp, hs, jnp.stack(kp), jnp.stack(vp), jnp.stack(ks_), jnp.stack(vs_),
            jnp.stack(dp), jnp.stack(ds), jnp.stack(cp), jnp.stack(cs))
```

```python
import functools
import math

import jax
import jax.numpy as jnp
import numpy as np
from jax import lax
from jax.experimental import pallas as pl
from jax.experimental.pallas import tpu as pltpu

F32 = jnp.float32
BF16 = jnp.bfloat16

N_HEADS = 4
HEAD_DIM = 128
QK_DIM = 64
CONV_W = 4
PAGE = 128
N_BUCKETS = 32
MAX_DISTANCE = 128
LN_EPS = 1e-5
NORM_EPS = 1e-6
NEG_INF = -1e30

LANES = 128
SUBLANES = 8
VMEM_LIMIT = 56 * 1024 * 1024

NN = (((1,), (0,)), ((), ()))
NT = (((1,), (1,)), ((), ()))


def _dot(a, b, dims=NN):
    return lax.dot_general(a, b, dims, preferred_element_type=F32)


def _split(a):
    hi = a.astype(BF16)
    lo = (a - hi.astype(F32)).astype(BF16)
    return hi, lo


def _dot3(a, b, dims=NN):
    ah, al = _split(a)
    bh, bl = _split(b)
    return _dot(ah, bh, dims) + (_dot(ah, bl, dims) + _dot(al, bh, dims))


def _dot2(a_exact, b, dims=NN):
    bh, bl = _split(b)
    a = a_exact.astype(BF16)
    return _dot(a, bh, dims) + _dot(a, bl, dims)


def _silu(x):
    return x * (1.0 / (1.0 + jnp.exp(-x)))


def _sigmoid(x):
    return 1.0 / (1.0 + jnp.exp(-x))


def _softplus(x):
    return jnp.maximum(x, 0.0) + jnp.log(1.0 + jnp.exp(-jnp.abs(x)))


def _cparams(sem):
    return pltpu.CompilerParams(dimension_semantics=sem, vmem_limit_bytes=VMEM_LIMIT)


def _bucket_np(dist):
    dist = np.asarray(dist, np.int64)
    n = np.maximum(dist, 0)
    max_exact = N_BUCKETS // 2
    nf = np.maximum(n, 1).astype(np.float32)
    large = max_exact + (np.log(nf / np.float32(max_exact)) / np.float32(math.log(MAX_DISTANCE / max_exact))
                         * np.float32(N_BUCKETS - max_exact)).astype(np.int32)
    large = np.minimum(large, N_BUCKETS - 1)
    b = np.where(n < max_exact, n, large)
    return np.where(dist < 0, -1, b).astype(np.int32)


def _prep_kernel(table_ref, bkt_p_ref, bkt_sp_ref, bkt_sn_ref, lq1_ref, lk1_ref, lq2_ref, lk2_ref,
                 bias_p_ref, bias_sp_ref, bias_sn_ref, lam_ref, *, depth):
    def lookup(bkt, h):
        far = table_ref[N_BUCKETS - 1, h]
        out = jnp.full(bkt.shape, NEG_INF, F32)
        for b in range(N_BUCKETS):
            out = jnp.where(bkt == b, table_ref[b, h] - far, out)
        return out

    for h in range(N_HEADS):
        for kind in range(2):
            bias_p_ref[h, kind] = lookup(bkt_p_ref[kind], h)
        bias_sp_ref[h * 8:(h + 1) * 8, :] = lookup(bkt_sp_ref[...], h)
        bias_sn_ref[h * 8:(h + 1) * 8, :] = lookup(bkt_sn_ref[...], h)

    rows = lax.broadcasted_iota(jnp.int32, (SUBLANES, LANES), 0)
    lam = jnp.zeros((SUBLANES, LANES), F32)
    for l in range(depth):
        lam_init = 0.8 - 0.6 * math.exp(-0.3 * l)
        s1 = jnp.sum(lq1_ref[l:l + 1, :] * lk1_ref[l:l + 1, :], axis=1, keepdims=True)
        s2 = jnp.sum(lq2_ref[l:l + 1, :] * lk2_ref[l:l + 1, :], axis=1, keepdims=True)
        val = jnp.exp(s1) - jnp.exp(s2) + lam_init
        lam = jnp.where(rows == l, val, lam)
    lam_ref[...] = lam


def _prep(table, lq1, lk1, lq2, lk2, t, depth):
    r = np.arange(t)[:, None]
    c = np.arange(t)[None, :]
    bkt_p = np.stack([_bucket_np(r - c), _bucket_np(t + r - c)])
    tq = np.tile(np.arange(4), 2)[:, None]
    bkt_sp = _bucket_np(PAGE + tq - np.arange(PAGE)[None, :])
    cn = np.arange(LANES)[None, :]
    bkt_sn = np.where(cn < 4, _bucket_np(tq - cn), -1).astype(np.int32)
    vm = pl.BlockSpec(memory_space=pltpu.VMEM)
    return pl.pallas_call(
        functools.partial(_prep_kernel, depth=depth),
        out_shape=(jax.ShapeDtypeStruct((N_HEADS, 2, t, t), F32),
                   jax.ShapeDtypeStruct((N_HEADS * 8, PAGE), F32),
                   jax.ShapeDtypeStruct((N_HEADS * 8, LANES), F32),
                   jax.ShapeDtypeStruct((SUBLANES, LANES), F32)),
        in_specs=[pl.BlockSpec(memory_space=pltpu.SMEM), vm, vm, vm, vm, vm, vm, vm],
        out_specs=(vm, vm, vm, vm),
        compiler_params=pltpu.CompilerParams(vmem_limit_bytes=VMEM_LIMIT),
        name="prep",
    )(table, jnp.asarray(bkt_p), jnp.asarray(bkt_sp), jnp.asarray(bkt_sn), lq1, lk1, lq2, lk2)


def _inproj_kernel(x_ref, w_ref, wbd_ref, *out_refs, mix, stacked_q):
    if stacked_q:
        q_ref, k_ref, kb_ref, v_ref, vb_ref, ag_ref, pre_ref, dg_ref, bd_ref = out_refs
    else:
        q_ref, k_ref, v_ref, ag_ref, pre_ref, dg_ref, bd_ref = out_refs
    xb = x_ref[...].astype(BF16)

    def seg(i, n=1):
        return _dot(xb, w_ref[:, i * mix:(i + n) * mix])

    q = seg(0) * (QK_DIM ** -0.5)
    if stacked_q:
        lane = lax.broadcasted_iota(jnp.int32, q.shape, 1)
        first = (lane % HEAD_DIM) < QK_DIM
        q_ref[0] = jnp.where(first, q, 0.0).astype(BF16)
        q_ref[1] = jnp.where(first, 0.0, q).astype(BF16)
    else:
        q_ref[...] = q
    k = seg(1)
    k_ref[...] = k
    v = seg(2)
    v_ref[...] = v
    if stacked_q:
        kb_ref[...] = k.astype(BF16)
        vb_ref[...] = v.astype(BF16)
    ag_ref[...] = _silu(seg(3)).astype(ag_ref.dtype)
    pre_ref[...] = seg(4, 3)
    dg_ref[...] = _silu(seg(7)).astype(dg_ref.dtype)
    bd_ref[...] = _dot(xb, wbd_ref[...])


def _inproj(x, w_main, w_bd, *, stacked_q):
    m, d = x.shape
    mix = d // 2
    tm = min(512, m)
    row = lambda i: (i, 0)
    full = lambda i: (0, 0)
    o_mix = pl.BlockSpec((tm, mix), row)
    if stacked_q:
        shapes = [jax.ShapeDtypeStruct((2, m, mix), BF16), jax.ShapeDtypeStruct((m, mix), F32),
                  jax.ShapeDtypeStruct((m, mix), BF16), jax.ShapeDtypeStruct((m, mix), F32),
                  jax.ShapeDtypeStruct((m, mix), BF16)]
        specs = [pl.BlockSpec((2, tm, mix), lambda i: (0, i, 0)), o_mix, o_mix, o_mix, o_mix]
    else:
        shapes = [jax.ShapeDtypeStruct((m, mix), F32)] * 3
        specs = [o_mix] * 3
    gate_dtype = BF16 if stacked_q else F32
    shapes += [jax.ShapeDtypeStruct((m, mix), gate_dtype), jax.ShapeDtypeStruct((m, 3 * mix), F32),
               jax.ShapeDtypeStruct((m, mix), gate_dtype), jax.ShapeDtypeStruct((m, LANES), F32)]
    specs += [o_mix, pl.BlockSpec((tm, 3 * mix), row), o_mix, pl.BlockSpec((tm, LANES), row)]
    return pl.pallas_call(
        functools.partial(_inproj_kernel, mix=mix, stacked_q=stacked_q),
        out_shape=tuple(shapes),
        grid=(m // tm,),
        in_specs=[pl.BlockSpec((tm, d), row), pl.BlockSpec(w_main.shape, full), pl.BlockSpec(w_bd.shape, full)],
        out_specs=tuple(specs),
        compiler_params=_cparams(("parallel",)),
        name="inproj",
    )(x, w_main, w_bd)


def _head_norm_gate(o, gain, gate, scale):
    ms = jnp.mean(o * o, axis=-1, keepdims=True)
    return o * lax.rsqrt(ms + NORM_EPS) * (gain * scale) * gate


def _attn_prompt_kernel(ii_ref, jj_ref, q_ref, k_ref, v_ref, bias_ref, ag_ref, lam_ref, gain_ref,
                        o_ref, m_ref, l_ref, acc_ref, *, t, layer, out_scale):
    s_idx = pl.program_id(2)
    i = ii_ref[s_idx]
    j = jj_ref[s_idx]

    @pl.when(j == 0)
    def _():
        m_ref[...] = jnp.full(m_ref.shape, NEG_INF, F32)
        l_ref[...] = jnp.zeros(l_ref.shape, F32)
        acc_ref[...] = jnp.zeros(acc_ref.shape, F32)

    def update(bias):
        q = q_ref[...].reshape(2 * t, HEAD_DIM)
        s = _dot(q, k_ref[...], NT)
        if bias is not None:
            s = s + jnp.concatenate([bias, bias], axis=0)
        m_prev = m_ref[...]
        m_new = jnp.maximum(m_prev, jnp.max(s, axis=1, keepdims=True))
        alpha = jnp.exp(m_prev - m_new)
        p = jnp.exp(s - m_new)
        l_ref[...] = alpha * l_ref[...] + jnp.sum(p, axis=1, keepdims=True)
        acc_ref[...] = alpha * acc_ref[...] + _dot(p.astype(BF16), v_ref[...])
        m_ref[...] = m_new

    @pl.when(j < i - 1)
    def _():
        update(None)

    @pl.when(j == i - 1)
    def _():
        update(bias_ref[1])

    @pl.when(j == i)
    def _():
        update(bias_ref[0])
        o = acc_ref[...] / l_ref[...]
        lam = lam_ref[layer:layer + 1, :]
        od = o[:t] - lam * o[t:]
        o_ref[...] = _head_norm_gate(od, gain_ref[...], ag_ref[...].astype(F32), out_scale).astype(o_ref.dtype)


def _attn_prompt(qst, kb, vb, bias_p, ag, lam, gain, *, batch, t, layer):
    m = kb.shape[0]
    n_t = m // batch // t
    ii = np.concatenate([np.full(i + 1, i) for i in range(n_t)]).astype(np.int32)
    jj = np.concatenate([np.arange(i + 1) for i in range(n_t)]).astype(np.int32)
    lam_init = 0.8 - 0.6 * math.exp(-0.3 * layer)
    q_map = lambda b, h, s, ii, jj: (0, b * n_t + ii[s], h)
    kv_map = lambda b, h, s, ii, jj: (b * n_t + jj[s], h)
    o_map = lambda b, h, s, ii, jj: (b * n_t + ii[s], h)
    grid_spec = pltpu.PrefetchScalarGridSpec(
        num_scalar_prefetch=2,
        grid=(batch, N_HEADS, len(ii)),
        in_specs=[pl.BlockSpec((2, t, HEAD_DIM), q_map),
                  pl.BlockSpec((t, HEAD_DIM), kv_map),
                  pl.BlockSpec((t, HEAD_DIM), kv_map),
                  pl.BlockSpec((None, 2, t, t), lambda b, h, s, ii, jj: (h, 0, 0, 0)),
                  pl.BlockSpec((t, HEAD_DIM), o_map),
                  pl.BlockSpec((SUBLANES, LANES), lambda b, h, s, ii, jj: (0, 0)),
                  pl.BlockSpec((1, HEAD_DIM), lambda b, h, s, ii, jj: (0, 0))],
        out_specs=pl.BlockSpec((t, HEAD_DIM), o_map),
        scratch_shapes=[pltpu.VMEM((2 * t, 1), F32), pltpu.VMEM((2 * t, 1), F32),
                        pltpu.VMEM((2 * t, HEAD_DIM), F32)],
    )
    return pl.pallas_call(
        functools.partial(_attn_prompt_kernel, t=t, layer=layer, out_scale=1.0 - lam_init),
        out_shape=jax.ShapeDtypeStruct((m, N_HEADS * HEAD_DIM), BF16),
        grid_spec=grid_spec,
        compiler_params=_cparams(("parallel", "parallel", "arbitrary")),
        name="attn_prompt",
    )(jnp.asarray(ii), jnp.asarray(jj), qst, kb, vb, bias_p, ag, lam, gain)


def _attn_sample_kernel(*refs, g, layer, out_scale):
    pt_ref = refs[0]
    q_ref, kn_ref, vn_ref = refs[1:4]
    k_refs = refs[4:4 + g]
    v_refs = refs[4 + g:4 + 2 * g]
    bias_sp_ref, bias_sn_ref, ag_ref, lam_ref, gain_ref = refs[4 + 2 * g:9 + 2 * g]
    o_ref = refs[9 + 2 * g]
    wq_ref, m_ref, l_ref, acc_ref, own_ref = refs[10 + 2 * g:]
    del pt_ref
    step = pl.program_id(1)
    n_steps = pl.num_programs(1)
    rows = N_HEADS * 8
    width = N_HEADS * HEAD_DIM

    @pl.when(step == 0)
    def _():
        r = lax.broadcasted_iota(jnp.int32, (rows, width), 0)
        c = lax.broadcasted_iota(jnp.int32, (rows, width), 1)
        keep = (c // HEAD_DIM == r // 8) & ((c % HEAD_DIM) // QK_DIM == (r // 4) % 2)
        wq = jnp.zeros((rows, width), F32)
        for tq in range(4):
            wq = jnp.where(keep & (r % 4 == tq), q_ref[tq:tq + 1, :], wq)
        wq_ref[...] = wq.astype(BF16)
        m_ref[...] = jnp.full(m_ref.shape, NEG_INF, F32)
        l_ref[...] = jnp.zeros(l_ref.shape, F32)
        acc_ref[...] = jnp.zeros(acc_ref.shape, F32)
        own_ref[...] = jnp.zeros(own_ref.shape, F32)

    def update(s, v_list):
        m_prev = m_ref[...]
        m_new = jnp.maximum(m_prev, jnp.max(s, axis=1, keepdims=True))
        alpha = jnp.exp(m_prev - m_new)
        p = jnp.exp(s - m_new)
        l_ref[...] = alpha * l_ref[...] + jnp.sum(p, axis=1, keepdims=True)
        p = p.astype(BF16)
        pv = None
        for idx, vb in enumerate(v_list):
            w = vb.shape[0]
            part = _dot(p[:, idx * PAGE:idx * PAGE + w] if w == PAGE else p[:, :w], vb)
            pv = part if pv is None else pv + part
        acc_ref[...] = alpha * acc_ref[...] + pv
        m_ref[...] = m_new

    wq = wq_ref[...]

    def page_scores():
        return [_dot(wq, k_refs[idx][...].astype(BF16), NT) for idx in range(g)]

    @pl.when(step < n_steps - 1)
    def _():
        update(jnp.concatenate(page_scores(), axis=1), [v_refs[idx][...].astype(BF16) for idx in range(g)])

    @pl.when(step == n_steps - 1)
    def _():
        sc = page_scores()
        sc[g - 1] = sc[g - 1] + bias_sp_ref[...]
        update(jnp.concatenate(sc, axis=1), [v_refs[idx][...].astype(BF16) for idx in range(g)])
        own_ref[0, 0:4, :] = kn_ref[...]
        own_ref[1, 0:4, :] = vn_ref[...]
        update(_dot(wq, own_ref[0].astype(BF16), NT) + bias_sn_ref[...], [own_ref[1].astype(BF16)])
        o = acc_ref[...] / l_ref[...]
        lam = lam_ref[layer:layer + 1, :]
        for h in range(N_HEADS):
            blk = o[h * 8:(h + 1) * 8, h * HEAD_DIM:(h + 1) * HEAD_DIM]
            od = (blk - lam * pltpu.roll(blk, 4, axis=0))[0:4]
            gate = ag_ref[:, h * HEAD_DIM:(h + 1) * HEAD_DIM].astype(F32)
            o_ref[:, h * HEAD_DIM:(h + 1) * HEAD_DIM] = _head_norm_gate(
                od, gain_ref[...], gate, out_scale).astype(o_ref.dtype)


def _attn_sample(q, k_new, v_new, cache_k, cache_v, page_table, bias_sp, bias_sn, ag, lam, gain, *, layer):
    bsz, tq, width = q.shape
    n_pages = page_table.shape[1]
    g = math.gcd(n_pages, 8)
    lam_init = 0.8 - 0.6 * math.exp(-0.3 * layer)
    per_b = lambda b, s, pt: (b, 0, 0)
    const2 = lambda b, s, pt: (0, 0)

    def page_spec(idx):
        return pl.BlockSpec((None, None, PAGE, width), lambda b, s, pt: (layer, pt[b, s * g + idx], 0, 0))

    grid_spec = pltpu.PrefetchScalarGridSpec(
        num_scalar_prefetch=1,
        grid=(bsz, n_pages // g),
        in_specs=([pl.BlockSpec((None, tq, width), per_b)] * 3
                  + [page_spec(idx) for idx in range(g)] * 2
                  + [pl.BlockSpec(bias_sp.shape, const2), pl.BlockSpec(bias_sn.shape, const2),
                     pl.BlockSpec((None, tq, width), per_b),
                     pl.BlockSpec((SUBLANES, LANES), const2), pl.BlockSpec((1, HEAD_DIM), const2)]),
        out_specs=pl.BlockSpec((None, tq, width), per_b),
        scratch_shapes=[pltpu.VMEM((N_HEADS * 8, width), BF16), pltpu.VMEM((N_HEADS * 8, 1), F32),
                        pltpu.VMEM((N_HEADS * 8, 1), F32), pltpu.VMEM((N_HEADS * 8, width), F32),
                        pltpu.VMEM((2, PAGE, width), F32)],
    )
    return pl.pallas_call(
        functools.partial(_attn_sample_kernel, g=g, layer=layer, out_scale=1.0 - lam_init),
        out_shape=jax.ShapeDtypeStruct((bsz, tq, width), F32),
        grid_spec=grid_spec,
        compiler_params=_cparams(("parallel", "arbitrary")),
        name="attn_sample",
    )(page_table, q, k_new, v_new, *([cache_k] * g), *([cache_v] * g), bias_sp, bias_sn, ag, lam, gain)


def _unit_lower_inverse(lmat, c):
    r = lax.broadcasted_iota(jnp.int32, (c, c), 0)
    q = lax.broadcasted_iota(jnp.int32, (c, c), 1)
    eye = (r == q).astype(F32)
    base = min(c, 16)
    neg = jnp.where(r // base == q // base, -lmat, 0.0)
    x = eye + neg
    p = neg
    for _ in range(int(math.log2(base)) - 1):
        p = _dot3(p, p)
        x = x + _dot3(x, p)
    s = base
    while s < c:
        off = jnp.where((r // (2 * s) == q // (2 * s)) & (r // s != q // s), lmat, 0.0)
        x = x - _dot3(x, _dot3(off, x))
        s *= 2
    return x


def _delta_kernel(pre_ref, bd_ref, dg_ref, buf_ref, cw_ref, hp_ref, s0_ref, gain_ref,
                  o_ref, s_out_ref, conv_out_ref, s_ref, xcat_ref, *, c, mix, t_valid):
    n = pl.program_id(1)
    n_chunks = pl.num_programs(1)

    @pl.when(n == 0)
    def _():
        s_ref[...] = s0_ref[...]
        xcat_ref[0:8, :] = buf_ref[...]

    xcat_ref[8:8 + c, :] = pre_ref[...]
    y = xcat_ref[5:5 + c, :] * cw_ref[0:1, :]
    for j in range(1, CONV_W):
        y = y + xcat_ref[5 + j:5 + j + c, :] * cw_ref[j:j + 1, :]
    y = _silu(y)

    last_valid = t_valid - (t_valid - 1) // c * c
    conv_out_ref[...] = xcat_ref[last_valid:last_valid + 8, :]
    xcat_ref[0:8, :] = xcat_ref[c:c + 8, :]

    bd = bd_ref[...]
    beta_t = _sigmoid(bd)
    g_t = -jnp.exp(hp_ref[0:1, :]) * _softplus(bd + hp_ref[1:2, :])
    if t_valid % c:
        row_ok = (lax.broadcasted_iota(jnp.int32, bd.shape, 0) + n * c) < t_valid
        beta_t = jnp.where(row_ok, beta_t, 0.0)
        g_t = jnp.where(row_ok, g_t, 0.0)

    r = lax.broadcasted_iota(jnp.int32, (c, c), 0)
    q_i = lax.broadcasted_iota(jnp.int32, (c, c), 1)
    incl = r >= q_i
    strict = r > q_i
    gcum_t = _dot2(incl.astype(F32), g_t)
    sel = (lax.broadcasted_iota(jnp.int32, (8, LANES), 1) == lax.broadcasted_iota(jnp.int32, (8, LANES), 0) + 4)
    gcum_rows = _dot2(sel.astype(F32), gcum_t, NT)
    eye_h = (lax.broadcasted_iota(jnp.int32, (HEAD_DIM, HEAD_DIM), 0)
             == lax.broadcasted_iota(jnp.int32, (HEAD_DIM, HEAD_DIM), 1)).astype(BF16)

    for h in range(N_HEADS):
        lo, hi = h * HEAD_DIM, (h + 1) * HEAD_DIM
        qh = y[:, lo:hi]
        kh = y[:, mix + lo:mix + hi]
        vh = y[:, 2 * mix + lo:2 * mix + hi]
        qn = qh * lax.rsqrt(jnp.sum(qh * qh, axis=-1, keepdims=True) + NORM_EPS) * (HEAD_DIM ** -0.5)
        kn = kh * lax.rsqrt(jnp.sum(kh * kh, axis=-1, keepdims=True) + NORM_EPS)
        beta = beta_t[:, h:h + 1]
        gcum = gcum_t[:, 4 + h:5 + h]
        g_last = gcum_t[c - 1:c, 4 + h:5 + h]
        diff = gcum - gcum_rows[h:h + 1, :]
        decay = jnp.where(incl, jnp.exp(jnp.where(incl, diff, 0.0)), 0.0)
        kb = kn * beta
        knb = kn.astype(BF16)
        lmat = jnp.where(strict, _dot(kb.astype(BF16), knb, NT) * decay, 0.0)
        a_qk = jnp.where(incl, _dot(qn.astype(BF16), knb, NT) * decay, 0.0)
        tinv = _unit_lower_inverse(lmat, c)
        e_g = jnp.exp(gcum)
        uw = _dot3(tinv, jnp.concatenate([vh * beta, kb * e_g], axis=1))
        u = uw[:, :HEAD_DIM]
        w = uw[:, HEAD_DIM:]
        q_dec = qn * e_g
        k_tail = kn * jnp.exp(g_last - gcum)

        s_h = s_ref[h]
        s_b = s_h.astype(BF16)
        v_new = u - _dot(w.astype(BF16), s_b)
        vnb = v_new.astype(BF16)
        o = _dot(q_dec.astype(BF16), s_b) + _dot(a_qk.astype(BF16), vnb)
        k_tail_t = _dot(eye_h, k_tail.astype(BF16), NT)
        s_ref[h] = s_h * jnp.exp(g_last) + _dot(k_tail_t.astype(BF16), vnb)

        ms = jnp.mean(o * o, axis=-1, keepdims=True)
        o_n = o * lax.rsqrt(ms + NORM_EPS) * gain_ref[...]
        o_ref[:, lo:hi] = (o_n * dg_ref[:, lo:hi].astype(F32)).astype(o_ref.dtype)

    @pl.when(n == n_chunks - 1)
    def _():
        s_out_ref[...] = s_ref[...]


def _delta(pre, bd, dg, conv_buf, conv_w, hp, s0, gain, *, batch, c, t_valid):
    m, width3 = pre.shape
    mix = width3 // 3
    t_pad = m // batch
    n_chunks = t_pad // c
    row = lambda b, n: (b * n_chunks + n, 0)
    per_b3 = lambda b, n: (b, 0, 0)
    o_d, s_new, conv_out = pl.pallas_call(
        functools.partial(_delta_kernel, c=c, mix=mix, t_valid=t_valid),
        out_shape=(jax.ShapeDtypeStruct((m, mix), dg.dtype),
                   jax.ShapeDtypeStruct(s0.shape, F32),
                   jax.ShapeDtypeStruct((batch, 8, width3), F32)),
        grid=(batch, n_chunks),
        in_specs=[pl.BlockSpec((c, width3), row), pl.BlockSpec((c, LANES), row), pl.BlockSpec((c, mix), row),
                  pl.BlockSpec((None, 8, width3), per_b3),
                  pl.BlockSpec((8, width3), lambda b, n: (0, 0)),
                  pl.BlockSpec((8, LANES), lambda b, n: (0, 0)),
                  pl.BlockSpec((None, N_HEADS, HEAD_DIM, HEAD_DIM), lambda b, n: (b, 0, 0, 0)),
                  pl.BlockSpec((1, HEAD_DIM), lambda b, n: (0, 0))],
        out_specs=(pl.BlockSpec((c, mix), row),
                   pl.BlockSpec((None, N_HEADS, HEAD_DIM, HEAD_DIM), lambda b, n: (b, 0, 0, 0)),
                   pl.BlockSpec((None, 8, width3), per_b3)),
        scratch_shapes=[pltpu.VMEM((N_HEADS, HEAD_DIM, HEAD_DIM), F32), pltpu.VMEM((c + 8, width3), F32)],
        compiler_params=_cparams(("parallel", "arbitrary")),
        name="delta",
    )(pre, bd, dg, conv_buf, conv_w, hp, s0, gain)
    return o_d, s_new, conv_out[:, 5:8, :]


def _post_kernel(x_ref, oa_ref, od_ref, wmg_ref, wbr_ref, wo_ref, lng_ref, lnb_ref, y_ref, *, d, alpha):
    x = x_ref[...]
    gates = _sigmoid(_dot(x.astype(BF16), wmg_ref[...]))
    h_a = _dot(oa_ref[...].astype(BF16), wbr_ref[0])
    h_d = _dot(od_ref[...].astype(BF16), wbr_ref[1])
    merged = gates[:, :d] * h_a + gates[:, d:] * h_d
    y = _dot(merged.astype(BF16), wo_ref[...])
    z = alpha * x + y
    mu = jnp.mean(z, axis=-1, keepdims=True)
    zc = z - mu
    var = jnp.mean(zc * zc, axis=-1, keepdims=True)
    y_ref[...] = zc * lax.rsqrt(var + LN_EPS) * lng_ref[...] + lnb_ref[...]


def _post(x, o_a, o_d, w_mg, w_br, w_o, ln_g, ln_b, *, alpha):
    m, d = x.shape
    tm = min(512, m)
    row = lambda i: (i, 0)
    c2 = lambda i: (0, 0)
    return pl.pallas_call(
        functools.partial(_post_kernel, d=d, alpha=alpha),
        out_shape=jax.ShapeDtypeStruct((m, d), F32),
        grid=(m // tm,),
        in_specs=[pl.BlockSpec((tm, d), row), pl.BlockSpec((tm, d // 2), row), pl.BlockSpec((tm, d // 2), row),
                  pl.BlockSpec(w_mg.shape, c2), pl.BlockSpec(w_br.shape, lambda i: (0, 0, 0)),
                  pl.BlockSpec(w_o.shape, c2), pl.BlockSpec((1, d), c2), pl.BlockSpec((1, d), c2)],
        out_specs=pl.BlockSpec((tm, d), row),
        compiler_params=_cparams(("parallel",)),
        name="post",
    )(x, o_a, o_d, w_mg, w_br, w_o, ln_g, ln_b)


def kernel(x_prompt, x_sample, cache_k, cache_v, state_delta, state_conv, page_table, rel_bias_table,
           w_in, conv_w, lambda_q1, lambda_k1, lambda_q2, lambda_k2, diff_norm_g, a_log, dt_bias,
           delta_norm_g, w_branch, w_o, ln_g, ln_b):
    batch, seq, d = x_prompt.shape
    dec_b, dec_t, _ = x_sample.shape
    depth = w_in.shape[0]
    mix = d // 2
    assert mix == N_HEADS * HEAD_DIM and dec_t == 4 and cache_k.shape[2] == PAGE
    alpha = (2 * depth) ** 0.25
    t_attn = min(256, seq)
    c_prompt = min(64, seq)
    c_sample = 8
    n_pool = cache_k.shape[1]

    bias_p, bias_sp, bias_sn, lam = _prep(rel_bias_table, lambda_q1, lambda_k1, lambda_q2, lambda_k2,
                                          t_attn, depth)
    cache_k2 = cache_k.reshape(depth, n_pool, PAGE, mix)
    cache_v2 = cache_v.reshape(depth, n_pool, PAGE, mix)

    hp = x_prompt.reshape(batch * seq, d)
    hs = x_sample.reshape(dec_b * dec_t, d)
    outs = {k: [] for k in ("kp", "vp", "ks", "vs", "dp", "ds", "cp", "cs")}
    zeros_buf = jnp.zeros((batch, 8, 3 * mix), F32)
    zeros_state = jnp.zeros((batch, N_HEADS, HEAD_DIM, HEAD_DIM), F32)

    for l in range(depth):
        w_main = w_in[l, :, :8 * mix].astype(BF16)
        w_bd = jnp.pad(w_in[l, :, 8 * mix:8 * mix + 2 * N_HEADS], ((0, 0), (0, LANES - 2 * N_HEADS))).astype(BF16)
        w_mg = w_in[l, :, 8 * mix + 2 * N_HEADS:].astype(BF16)
        w_br = w_branch[l].astype(BF16)
        w_ol = w_o[l].astype(BF16)
        cw = jnp.pad(conv_w[l], ((0, 8 - CONV_W), (0, 0)))
        hpar = jnp.zeros((8, LANES), F32)
        hpar = hpar.at[0, N_HEADS:2 * N_HEADS].set(a_log[l]).at[1, N_HEADS:2 * N_HEADS].set(dt_bias[l])
        gain_a = diff_norm_g[l][None, :]
        gain_d = delta_norm_g[l][None, :]
        lng = ln_g[l][None, :]
        lnb = ln_b[l][None, :]

        qst, k_f, kb, v_f, vb, ag, pre, dg, bd = _inproj(hp, w_main, w_bd, stacked_q=True)
        o_a = _attn_prompt(qst, kb, vb, bias_p, ag, lam, gain_a, batch=batch, t=t_attn, layer=l)
        o_d, s_new, c_new = _delta(pre, bd, dg, zeros_buf, cw, hpar, zeros_state, gain_d,
                                   batch=batch, c=c_prompt, t_valid=seq)
        hp = _post(hp, o_a, o_d, w_mg, w_br, w_ol, lng, lnb, alpha=alpha)
        outs["kp"].append(k_f.reshape(batch, seq, N_HEADS, HEAD_DIM))
        outs["vp"].append(v_f.reshape(batch, seq, N_HEADS, HEAD_DIM))
        outs["dp"].append(s_new)
        outs["cp"].append(c_new)

        q_s, k_s, v_s, ag_s, pre_s, dg_s, bd_s = _inproj(hs, w_main, w_bd, stacked_q=False)
        o_a = _attn_sample(q_s.reshape(dec_b, dec_t, mix), k_s.reshape(dec_b, dec_t, mix),
                           v_s.reshape(dec_b, dec_t, mix), cache_k2, cache_v2, page_table,
                           bias_sp, bias_sn, ag_s.reshape(dec_b, dec_t, mix), lam, gain_a, layer=l)

        def pad_rows(a):
            a = a.reshape(dec_b, dec_t, a.shape[-1])
            return jnp.pad(a, ((0, 0), (0, c_sample - dec_t), (0, 0))).reshape(dec_b * c_sample, a.shape[-1])

        buf_s = jnp.pad(state_conv[l], ((0, 0), (8 - (CONV_W - 1), 0), (0, 0)))
        o_d, s_new, c_new = _delta(pad_rows(pre_s), pad_rows(bd_s), pad_rows(dg_s), buf_s, cw, hpar,
                                   state_delta[l], gain_d, batch=dec_b, c=c_sample, t_valid=dec_t)
        o_d = o_d.reshape(dec_b, c_sample, mix)[:, :dec_t].reshape(dec_b * dec_t, mix)
        hs = _post(hs, o_a.reshape(dec_b * dec_t, mix), o_d, w_mg, w_br, w_ol, lng, lnb, alpha=alpha)
        outs["ks"].append(k_s.reshape(dec_b, dec_t, N_HEADS, HEAD_DIM))
        outs["vs"].append(v_s.reshape(dec_b, dec_t, N_HEADS, HEAD_DIM))
        outs["ds"].append(s_new)
        outs["cs"].append(c_new)

    st = {k: jnp.stack(v) for k, v in outs.items()}
    return (hp.reshape(batch, seq, d), hs.reshape(dec_b, dec_t, d), st["kp"], st["vp"], st["ks"], st["vs"],
            st["dp"], st["ds"], st["cp"], st["cs"])
```

```python
import functools
import math

import jax
import jax.numpy as jnp
import numpy as np
from jax import lax
from jax.experimental import pallas as pl
from jax.experimental.pallas import tpu as pltpu

F32 = jnp.float32
BF16 = jnp.bfloat16

N_HEADS = 4
HEAD_DIM = 128
QK_DIM = 64
CONV_W = 4
PAGE = 128
N_BUCKETS = 32
MAX_DISTANCE = 128
LN_EPS = 1e-5
NORM_EPS = 1e-6
NEG_INF = -1e30

LANES = 128
SUBLANES = 8
VMEM_LIMIT = 56 * 1024 * 1024

NN = (((1,), (0,)), ((), ()))
NT = (((1,), (1,)), ((), ()))


def _dot(a, b, dims=NN):
    return lax.dot_general(a, b, dims, preferred_element_type=F32)


def _split(a):
    hi = a.astype(BF16)
    lo = (a - hi.astype(F32)).astype(BF16)
    return hi, lo


def _dot3(a, b, dims=NN):
    ah, al = _split(a)
    bh, bl = _split(b)
    return _dot(ah, bh, dims) + (_dot(ah, bl, dims) + _dot(al, bh, dims))


def _dot2(a_exact, b, dims=NN):
    bh, bl = _split(b)
    a = a_exact.astype(BF16)
    return _dot(a, bh, dims) + _dot(a, bl, dims)


def _silu(x):
    return x * (1.0 / (1.0 + jnp.exp(-x)))


def _sigmoid(x):
    return 1.0 / (1.0 + jnp.exp(-x))


def _softplus(x):
    return jnp.maximum(x, 0.0) + jnp.log(1.0 + jnp.exp(-jnp.abs(x)))


def _cparams(sem):
    return pltpu.CompilerParams(dimension_semantics=sem, vmem_limit_bytes=VMEM_LIMIT)


def _bucket_np(dist):
    dist = np.asarray(dist, np.int64)
    n = np.maximum(dist, 0)
    max_exact = N_BUCKETS // 2
    nf = np.maximum(n, 1).astype(np.float32)
    large = max_exact + (np.log(nf / np.float32(max_exact)) / np.float32(math.log(MAX_DISTANCE / max_exact))
                         * np.float32(N_BUCKETS - max_exact)).astype(np.int32)
    large = np.minimum(large, N_BUCKETS - 1)
    b = np.where(n < max_exact, n, large)
    return np.where(dist < 0, -1, b).astype(np.int32)


def _prep_kernel(table_ref, bkt_p_ref, bkt_sp_ref, bkt_sn_ref, lq1_ref, lk1_ref, lq2_ref, lk2_ref,
                 bias_p_ref, bias_sp_ref, bias_sn_ref, lam_ref, *, depth):
    def lookup(bkt, h):
        far = table_ref[N_BUCKETS - 1, h]
        out = jnp.full(bkt.shape, NEG_INF, F32)
        for b in range(N_BUCKETS):
            out = jnp.where(bkt == b, table_ref[b, h] - far, out)
        return out

    for h in range(N_HEADS):
        for kind in range(2):
            bias_p_ref[h, kind] = lookup(bkt_p_ref[kind], h)
        bias_sp_ref[h * 8:(h + 1) * 8, :] = lookup(bkt_sp_ref[...], h)
        bias_sn_ref[h * 8:(h + 1) * 8, :] = lookup(bkt_sn_ref[...], h)

    rows = lax.broadcasted_iota(jnp.int32, (SUBLANES, LANES), 0)
    lam = jnp.zeros((SUBLANES, LANES), F32)
    for l in range(depth):
        lam_init = 0.8 - 0.6 * math.exp(-0.3 * l)
        s1 = jnp.sum(lq1_ref[l:l + 1, :] * lk1_ref[l:l + 1, :], axis=1, keepdims=True)
        s2 = jnp.sum(lq2_ref[l:l + 1, :] * lk2_ref[l:l + 1, :], axis=1, keepdims=True)
        val = jnp.exp(s1) - jnp.exp(s2) + lam_init
        lam = jnp.where(rows == l, val, lam)
    lam_ref[...] = lam


def _prep(table, lq1, lk1, lq2, lk2, t, depth):
    key = np.arange(t)[:, None]
    qry = np.arange(t)[None, :]
    bkt_p = np.stack([_bucket_np(qry - key), _bucket_np(t + qry - key)])
    tq = np.tile(np.arange(4), 2)[:, None]
    bkt_sp = _bucket_np(PAGE + tq - np.arange(PAGE)[None, :])
    cn = np.arange(LANES)[None, :]
    bkt_sn = np.where(cn < 4, _bucket_np(tq - cn), -1).astype(np.int32)
    vm = pl.BlockSpec(memory_space=pltpu.VMEM)
    return pl.pallas_call(
        functools.partial(_prep_kernel, depth=depth),
        out_shape=(jax.ShapeDtypeStruct((N_HEADS, 2, t, t), F32),
                   jax.ShapeDtypeStruct((N_HEADS * 8, PAGE), F32),
                   jax.ShapeDtypeStruct((N_HEADS * 8, LANES), F32),
                   jax.ShapeDtypeStruct((SUBLANES, LANES), F32)),
        in_specs=[pl.BlockSpec(memory_space=pltpu.SMEM), vm, vm, vm, vm, vm, vm, vm],
        out_specs=(vm, vm, vm, vm),
        compiler_params=pltpu.CompilerParams(vmem_limit_bytes=VMEM_LIMIT),
        name="prep",
    )(table, jnp.asarray(bkt_p), jnp.asarray(bkt_sp), jnp.asarray(bkt_sn), lq1, lk1, lq2, lk2)


def _inproj_kernel(x_ref, w_ref, wbd_ref, *refs, mix, tm, prompt, n_prev):
    if prompt:
        wt_ref = refs[0]
        qt_ref, k_ref, kb_ref, v_ref, vt_ref, ag_ref, pre_ref, dg_ref, bd_ref = refs[1 + n_prev:]
    else:
        q_ref, k_ref, v_ref, ag_ref, pre_ref, dg_ref, bd_ref = refs
    xb = x_ref[...].astype(BF16)

    def seg(i, n=1):
        return _dot(xb, w_ref[:, i * mix:(i + n) * mix])

    scale = QK_DIM ** -0.5
    k = seg(1)
    v = seg(2)
    if prompt:
        qt = _dot(wt_ref[0], xb, NT) * scale
        feat = lax.broadcasted_iota(jnp.int32, qt.shape, 0)
        first = (feat % HEAD_DIM) < QK_DIM
        qt_ref[:, :tm] = jnp.where(first, qt, 0.0).astype(BF16)
        qt_ref[:, tm:] = jnp.where(first, 0.0, qt).astype(BF16)
        vt_ref[...] = _dot(wt_ref[1], xb, NT).astype(BF16)
        kb_ref[...] = k.astype(BF16)
        for h in range(N_HEADS):
            k_ref[pl.ds(h, tm, stride=N_HEADS), :] = k[:, h * HEAD_DIM:(h + 1) * HEAD_DIM]
            v_ref[pl.ds(h, tm, stride=N_HEADS), :] = v[:, h * HEAD_DIM:(h + 1) * HEAD_DIM]
    else:
        q_ref[...] = seg(0) * scale
        k_ref[...] = k
        v_ref[...] = v
    ag_ref[...] = _silu(seg(3)).astype(ag_ref.dtype)
    pre_ref[...] = seg(4, 3)
    dg_ref[...] = _silu(seg(7)).astype(dg_ref.dtype)
    bd_ref[...] = _dot(xb, wbd_ref[...])


def _inproj(x, w_main, w_bd, w_t=None, kv_prev=None, *, tm, layer=0, depth=1):
    m, d = x.shape
    mix = d // 2
    prompt = w_t is not None
    row = lambda i: (i, 0)
    col = lambda i: (0, i)
    full = lambda i: (0, 0)
    o_mix = pl.BlockSpec((tm, mix), row)
    f_mix = jax.ShapeDtypeStruct((m, mix), F32)
    b_mix = jax.ShapeDtypeStruct((m, mix), BF16)
    in_specs = [pl.BlockSpec((tm, d), row), pl.BlockSpec(w_main.shape, full), pl.BlockSpec(w_bd.shape, full)]
    args = [x, w_main, w_bd]
    aliases = {}
    n_prev = 0
    if prompt:
        in_specs.append(pl.BlockSpec(w_t.shape, lambda i: (0, 0, 0)))
        args.append(w_t)
        if kv_prev is not None:
            n_prev = 2
            in_specs += [pl.BlockSpec(memory_space=pl.ANY)] * 2
            args += list(kv_prev)
            aliases = {4: 1, 5: 3}
        kv_all = jax.ShapeDtypeStruct((depth, m * N_HEADS, HEAD_DIM), F32)
        kv_spec = pl.BlockSpec((None, tm * N_HEADS, HEAD_DIM), lambda i: (layer, i, 0))
        shapes = [jax.ShapeDtypeStruct((mix, 2 * m), BF16), kv_all, b_mix, kv_all,
                  jax.ShapeDtypeStruct((mix, m), BF16)]
        specs = [pl.BlockSpec((mix, 2 * tm), col), kv_spec, o_mix, kv_spec, pl.BlockSpec((mix, tm), col)]
    else:
        shapes = [f_mix] * 3
        specs = [o_mix] * 3
    gate = b_mix if prompt else f_mix
    shapes += [gate, jax.ShapeDtypeStruct((m, 3 * mix), F32), gate, jax.ShapeDtypeStruct((m, LANES), F32)]
    specs += [o_mix, pl.BlockSpec((tm, 3 * mix), row), o_mix, pl.BlockSpec((tm, LANES), row)]
    return pl.pallas_call(
        functools.partial(_inproj_kernel, mix=mix, tm=tm, prompt=prompt, n_prev=n_prev),
        out_shape=tuple(shapes),
        grid=(m // tm,),
        in_specs=in_specs,
        out_specs=tuple(specs),
        input_output_aliases=aliases,
        compiler_params=_cparams(("parallel",)),
        name="inproj",
    )(*args)


def _head_norm_gate(o, gain, gate, scale):
    ms = jnp.mean(o * o, axis=-1, keepdims=True)
    return o * lax.rsqrt(ms + NORM_EPS) * (gain * scale) * gate


def _attn_prompt_kernel(ii_ref, jj_ref, qt_ref, k_ref, vt_ref, bias_ref, ag_ref, lam_ref, gain_ref,
                        o_ref, m_ref, l_ref, acc_ref, *, t, gw, layer, out_scale):
    s_idx = pl.program_id(2)
    i = ii_ref[s_idx]
    j = jj_ref[s_idx]
    n_groups = 2 * t // gw

    @pl.when(j == 0)
    def _():
        m_ref[...] = jnp.full(m_ref.shape, NEG_INF, F32)
        l_ref[...] = jnp.zeros(l_ref.shape, F32)
        acc_ref[...] = jnp.zeros(acc_ref.shape, F32)

    def scores(g, kind):
        s = _dot(k_ref[...], qt_ref[:, g * gw:(g + 1) * gw])
        if kind is not None:
            q0 = (g * gw) % t
            s = s + bias_ref[kind, :, q0:q0 + gw]
        return s

    def update(kind):
        vt = vt_ref[...]
        s_next = scores(0, kind)
        for g in range(n_groups):
            s = s_next
            if g + 1 < n_groups:
                s_next = scores(g + 1, kind)
            cols = slice(g * gw, (g + 1) * gw)
            m_prev = m_ref[:, cols]
            m_new = jnp.maximum(m_prev, jnp.max(s, axis=0, keepdims=True))
            alpha = jnp.exp(m_prev - m_new)
            p = jnp.exp(s - m_new)
            l_ref[:, cols] = alpha * l_ref[:, cols] + jnp.sum(p, axis=0, keepdims=True)
            acc_ref[:, cols] = alpha * acc_ref[:, cols] + _dot(vt, p.astype(BF16))
            m_ref[:, cols] = m_new

    @pl.when(j < i - 1)
    def _():
        update(None)

    @pl.when(j == i - 1)
    def _():
        update(1)

    @pl.when(j == i)
    def _():
        update(0)
        o_t = acc_ref[...] * (1.0 / l_ref[...])
        lam = lam_ref[layer:layer + 1, 0:1]
        od = (o_t[:, :t] - lam * o_t[:, t:]).T
        o_ref[...] = _head_norm_gate(od, gain_ref[...], ag_ref[...].astype(F32), out_scale).astype(o_ref.dtype)


def _attn_prompt(qt, kb, vt, bias_p, ag, lam, gain, *, batch, t, layer):
    m = kb.shape[0]
    n_t = m // batch // t
    ii = np.concatenate([np.full(i + 1, i) for i in range(n_t)]).astype(np.int32)
    jj = np.concatenate([np.arange(i + 1) for i in range(n_t)]).astype(np.int32)
    lam_init = 0.8 - 0.6 * math.exp(-0.3 * layer)
    gw = min(t, 2 * LANES)
    q_map = lambda b, h, s, ii, jj: (h, b * n_t + ii[s])
    k_map = lambda b, h, s, ii, jj: (b * n_t + jj[s], h)
    v_map = lambda b, h, s, ii, jj: (h, b * n_t + jj[s])
    o_map = lambda b, h, s, ii, jj: (b * n_t + ii[s], h)
    grid_spec = pltpu.PrefetchScalarGridSpec(
        num_scalar_prefetch=2,
        grid=(batch, N_HEADS, len(ii)),
        in_specs=[pl.BlockSpec((HEAD_DIM, 2 * t), q_map),
                  pl.BlockSpec((t, HEAD_DIM), k_map),
                  pl.BlockSpec((HEAD_DIM, t), v_map),
                  pl.BlockSpec((None, 2, t, t), lambda b, h, s, ii, jj: (h, 0, 0, 0)),
                  pl.BlockSpec((t, HEAD_DIM), o_map),
                  pl.BlockSpec((SUBLANES, LANES), lambda b, h, s, ii, jj: (0, 0)),
                  pl.BlockSpec((1, HEAD_DIM), lambda b, h, s, ii, jj: (0, 0))],
        out_specs=pl.BlockSpec((t, HEAD_DIM), o_map),
        scratch_shapes=[pltpu.VMEM((1, 2 * t), F32), pltpu.VMEM((1, 2 * t), F32),
                        pltpu.VMEM((HEAD_DIM, 2 * t), F32)],
    )
    return pl.pallas_call(
        functools.partial(_attn_prompt_kernel, t=t, gw=gw, layer=layer, out_scale=1.0 - lam_init),
        out_shape=jax.ShapeDtypeStruct((m, N_HEADS * HEAD_DIM), BF16),
        grid_spec=grid_spec,
        compiler_params=_cparams(("parallel", "parallel", "arbitrary")),
        name="attn_prompt",
    )(jnp.asarray(ii), jnp.asarray(jj), qt, kb, vt, bias_p, ag, lam, gain)


def _attn_sample_kernel(*refs, g, layer, out_scale):
    pt_ref = refs[0]
    q_ref, kn_ref, vn_ref = refs[1:4]
    k_refs = refs[4:4 + g]
    v_refs = refs[4 + g:4 + 2 * g]
    bias_sp_ref, bias_sn_ref, ag_ref, lam_ref, gain_ref = refs[4 + 2 * g:9 + 2 * g]
    o_ref = refs[9 + 2 * g]
    wq_ref, m_ref, l_ref, acc_ref, own_ref = refs[10 + 2 * g:]
    del pt_ref
    step = pl.program_id(1)
    n_steps = pl.num_programs(1)
    rows = N_HEADS * 8
    width = N_HEADS * HEAD_DIM

    @pl.when(step == 0)
    def _():
        r = lax.broadcasted_iota(jnp.int32, (rows, width), 0)
        c = lax.broadcasted_iota(jnp.int32, (rows, width), 1)
        keep = (c // HEAD_DIM == r // 8) & ((c % HEAD_DIM) // QK_DIM == (r // 4) % 2)
        wq = jnp.zeros((rows, width), F32)
        for tq in range(4):
            wq = jnp.where(keep & (r % 4 == tq), q_ref[tq:tq + 1, :], wq)
        wq_ref[...] = wq.astype(BF16)
        m_ref[...] = jnp.full(m_ref.shape, NEG_INF, F32)
        l_ref[...] = jnp.zeros(l_ref.shape, F32)
        acc_ref[...] = jnp.zeros(acc_ref.shape, F32)
        own_ref[...] = jnp.zeros(own_ref.shape, F32)

    def update(s, v_list):
        m_prev = m_ref[...]
        m_new = jnp.maximum(m_prev, jnp.max(s, axis=1, keepdims=True))
        alpha = jnp.exp(m_prev - m_new)
        p = jnp.exp(s - m_new)
        l_ref[...] = alpha * l_ref[...] + jnp.sum(p, axis=1, keepdims=True)
        p = p.astype(BF16)
        pv = None
        for idx, vb in enumerate(v_list):
            w = vb.shape[0]
            part = _dot(p[:, idx * PAGE:idx * PAGE + w] if w == PAGE else p[:, :w], vb)
            pv = part if pv is None else pv + part
        acc_ref[...] = alpha * acc_ref[...] + pv
        m_ref[...] = m_new

    wq = wq_ref[...]

    def page(ref):
        return jnp.concatenate([ref[pl.ds(h, PAGE, stride=N_HEADS), :] for h in range(N_HEADS)],
                               axis=1).astype(BF16)

    def page_scores():
        return [_dot(wq, page(k_refs[idx]), NT) for idx in range(g)]

    @pl.when(step < n_steps - 1)
    def _():
        update(jnp.concatenate(page_scores(), axis=1), [page(v_refs[idx]) for idx in range(g)])

    @pl.when(step == n_steps - 1)
    def _():
        sc = page_scores()
        sc[g - 1] = sc[g - 1] + bias_sp_ref[...]
        update(jnp.concatenate(sc, axis=1), [page(v_refs[idx]) for idx in range(g)])
        own_ref[0, 0:4, :] = kn_ref[...]
        own_ref[1, 0:4, :] = vn_ref[...]
        update(_dot(wq, own_ref[0].astype(BF16), NT) + bias_sn_ref[...], [own_ref[1].astype(BF16)])
        o = acc_ref[...] / l_ref[...]
        lam = lam_ref[layer:layer + 1, :]
        for h in range(N_HEADS):
            blk = o[h * 8:(h + 1) * 8, h * HEAD_DIM:(h + 1) * HEAD_DIM]
            od = (blk - lam * pltpu.roll(blk, 4, axis=0))[0:4]
            gate = ag_ref[:, h * HEAD_DIM:(h + 1) * HEAD_DIM].astype(F32)
            o_ref[:, h * HEAD_DIM:(h + 1) * HEAD_DIM] = _head_norm_gate(
                od, gain_ref[...], gate, out_scale).astype(o_ref.dtype)


def _attn_sample(q, k_new, v_new, cache_k, cache_v, page_table, bias_sp, bias_sn, ag, lam, gain, *, layer):
    bsz, tq, width = q.shape
    n_pages = page_table.shape[1]
    g = math.gcd(n_pages, 16)
    lam_init = 0.8 - 0.6 * math.exp(-0.3 * layer)
    per_b = lambda b, s, pt: (b, 0, 0)
    const2 = lambda b, s, pt: (0, 0)

    def page_spec(idx):
        return pl.BlockSpec((None, None, PAGE * N_HEADS, HEAD_DIM),
                            lambda b, s, pt: (layer, pt[b, s * g + idx], 0, 0))

    grid_spec = pltpu.PrefetchScalarGridSpec(
        num_scalar_prefetch=1,
        grid=(bsz, n_pages // g),
        in_specs=([pl.BlockSpec((None, tq, width), per_b)] * 3
                  + [page_spec(idx) for idx in range(g)] * 2
                  + [pl.BlockSpec(bias_sp.shape, const2), pl.BlockSpec(bias_sn.shape, const2),
                     pl.BlockSpec((None, tq, width), per_b),
                     pl.BlockSpec((SUBLANES, LANES), const2), pl.BlockSpec((1, HEAD_DIM), const2)]),
        out_specs=pl.BlockSpec((None, tq, width), per_b),
        scratch_shapes=[pltpu.VMEM((N_HEADS * 8, width), BF16), pltpu.VMEM((N_HEADS * 8, 1), F32),
                        pltpu.VMEM((N_HEADS * 8, 1), F32), pltpu.VMEM((N_HEADS * 8, width), F32),
                        pltpu.VMEM((2, PAGE, width), F32)],
    )
    return pl.pallas_call(
        functools.partial(_attn_sample_kernel, g=g, layer=layer, out_scale=1.0 - lam_init),
        out_shape=jax.ShapeDtypeStruct((bsz, tq, width), F32),
        grid_spec=grid_spec,
        compiler_params=_cparams(("parallel", "arbitrary")),
        name="attn_sample",
    )(page_table, q, k_new, v_new, *([cache_k] * g), *([cache_v] * g), bias_sp, bias_sn, ag, lam, gain)


def _unit_lower_inverses(lmats, c):
    r = lax.broadcasted_iota(jnp.int32, (c, c), 0)
    q = lax.broadcasted_iota(jnp.int32, (c, c), 1)
    eye = (r == q).astype(F32)
    base = min(c, 16)
    same_base = r // base == q // base
    ps = [jnp.where(same_base, -lm, 0.0) for lm in lmats]
    xs = [eye + p for p in ps]
    for _ in range(int(math.log2(base)) - 1):
        ps = [_dot3(p, p) for p in ps]
        xs = [x + _dot3(x, p) for x, p in zip(xs, ps)]
    s = base
    while s < c:
        below = (r // (2 * s) == q // (2 * s)) & (r // s != q // s)
        ts = [_dot3(jnp.where(below, lm, 0.0), x) for lm, x in zip(lmats, xs)]
        xs = [x - _dot3(x, t) for x, t in zip(xs, ts)]
        s *= 2
    return xs


def _delta_kernel(pre_ref, bd_ref, dg_ref, buf_ref, cw_ref, hp_ref, s0_ref, gain_ref,
                  o_ref, s_out_ref, conv_out_ref, s_ref, xcat_ref, *, nb, c, mix, t_valid):
    n = pl.program_id(1)
    n_chunks = pl.num_programs(1)
    units = [(b, h) for b in range(nb) for h in range(N_HEADS)]

    @pl.when(n == 0)
    def _():
        s_ref[...] = s0_ref[...]
        xcat_ref[:, 0:8, :] = buf_ref[...]

    r = lax.broadcasted_iota(jnp.int32, (c, c), 0)
    q_i = lax.broadcasted_iota(jnp.int32, (c, c), 1)
    incl = r >= q_i
    strict = r > q_i
    incl_f = incl.astype(F32)
    sel = (lax.broadcasted_iota(jnp.int32, (8, LANES), 1)
           == lax.broadcasted_iota(jnp.int32, (8, LANES), 0) + 4).astype(F32)
    eye_h = (lax.broadcasted_iota(jnp.int32, (HEAD_DIM, HEAD_DIM), 0)
             == lax.broadcasted_iota(jnp.int32, (HEAD_DIM, HEAD_DIM), 1)).astype(BF16)
    last_valid = t_valid - (t_valid - 1) // c * c

    ys, beta_ts, gcum_ts, gcum_rows = [], [], [], []
    for b in range(nb):
        xcat_ref[b, 8:8 + c, :] = pre_ref[b]
        y = xcat_ref[b, 5:5 + c, :] * cw_ref[0:1, :]
        for j in range(1, CONV_W):
            y = y + xcat_ref[b, 5 + j:5 + j + c, :] * cw_ref[j:j + 1, :]
        ys.append(_silu(y))
        conv_out_ref[b] = xcat_ref[b, last_valid:last_valid + 8, :]
        xcat_ref[b, 0:8, :] = xcat_ref[b, c:c + 8, :]

        bd = bd_ref[b]
        beta_t = _sigmoid(bd)
        g_t = -jnp.exp(hp_ref[0:1, :]) * _softplus(bd + hp_ref[1:2, :])
        if t_valid % c:
            row_ok = (lax.broadcasted_iota(jnp.int32, bd.shape, 0) + n * c) < t_valid
            beta_t = jnp.where(row_ok, beta_t, 0.0)
            g_t = jnp.where(row_ok, g_t, 0.0)
        beta_ts.append(beta_t)
        gcum_ts.append(_dot2(incl_f, g_t))
    for b in range(nb):
        gcum_rows.append(_dot2(sel, gcum_ts[b], NT))

    qn, kn, vb_, kb, e_g, decay, g_last, gcum = {}, {}, {}, {}, {}, {}, {}, {}
    for u in units:
        b, h = u
        lo, hi = h * HEAD_DIM, (h + 1) * HEAD_DIM
        qh = ys[b][:, lo:hi]
        kh = ys[b][:, mix + lo:mix + hi]
        qn[u] = qh * lax.rsqrt(jnp.sum(qh * qh, axis=-1, keepdims=True) + NORM_EPS) * (HEAD_DIM ** -0.5)
        kn[u] = kh * lax.rsqrt(jnp.sum(kh * kh, axis=-1, keepdims=True) + NORM_EPS)
        beta = beta_ts[b][:, h:h + 1]
        gcum[u] = gcum_ts[b][:, 4 + h:5 + h]
        g_last[u] = gcum_ts[b][c - 1:c, 4 + h:5 + h]
        diff = gcum[u] - gcum_rows[b][h:h + 1, :]
        decay[u] = jnp.where(incl, jnp.exp(jnp.where(incl, diff, 0.0)), 0.0)
        kb[u] = kn[u] * beta
        vb_[u] = ys[b][:, 2 * mix + lo:2 * mix + hi] * beta
        e_g[u] = jnp.exp(gcum[u])
    knb = {u: kn[u].astype(BF16) for u in units}
    lmat = {u: jnp.where(strict, _dot(kb[u].astype(BF16), knb[u], NT) * decay[u], 0.0) for u in units}
    a_qk = {u: jnp.where(incl, _dot(qn[u].astype(BF16), knb[u], NT) * decay[u], 0.0).astype(BF16)
            for u in units}

    tinv = dict(zip(units, _unit_lower_inverses([lmat[u] for u in units], c)))
    uw = {u: _dot3(tinv[u], jnp.concatenate([vb_[u], kb[u] * e_g[u]], axis=1)) for u in units}

    s_old = {u: s_ref[u[0], u[1]] for u in units}
    s_b = {u: s_old[u].astype(BF16) for u in units}
    ws = {u: _dot(uw[u][:, HEAD_DIM:].astype(BF16), s_b[u]) for u in units}
    qs = {u: _dot((qn[u] * e_g[u]).astype(BF16), s_b[u]) for u in units}
    k_tail_t = {u: _dot(eye_h, (kn[u] * jnp.exp(g_last[u] - gcum[u])).astype(BF16), NT).astype(BF16)
                for u in units}
    vnb = {u: (uw[u][:, :HEAD_DIM] - ws[u]).astype(BF16) for u in units}
    o = {u: qs[u] + _dot(a_qk[u], vnb[u]) for u in units}
    for u in units:
        s_ref[u[0], u[1]] = s_old[u] * jnp.exp(g_last[u]) + _dot(k_tail_t[u], vnb[u])
    for u in units:
        b, h = u
        lo, hi = h * HEAD_DIM, (h + 1) * HEAD_DIM
        ms = jnp.mean(o[u] * o[u], axis=-1, keepdims=True)
        o_n = o[u] * lax.rsqrt(ms + NORM_EPS) * gain_ref[...]
        o_ref[b, :, lo:hi] = (o_n * dg_ref[b, :, lo:hi].astype(F32)).astype(o_ref.dtype)

    @pl.when(n == n_chunks - 1)
    def _():
        s_out_ref[...] = s_ref[...]


def _delta(pre, bd, dg, conv_buf, conv_w, hp, s0, gain, *, nb, c, t_valid):
    batch, t_pad, width3 = pre.shape
    mix = width3 // 3
    n_chunks = t_pad // c
    tok = lambda b, n: (b, n, 0)
    per_b3 = lambda b, n: (b, 0, 0)
    per_b4 = lambda b, n: (b, 0, 0, 0)
    const2 = lambda b, n: (0, 0)
    o_d, s_new, conv_out = pl.pallas_call(
        functools.partial(_delta_kernel, nb=nb, c=c, mix=mix, t_valid=t_valid),
        out_shape=(jax.ShapeDtypeStruct((batch, t_pad, mix), dg.dtype),
                   jax.ShapeDtypeStruct(s0.shape, F32),
                   jax.ShapeDtypeStruct((batch, 8, width3), F32)),
        grid=(batch // nb, n_chunks),
        in_specs=[pl.BlockSpec((nb, c, width3), tok), pl.BlockSpec((nb, c, LANES), tok),
                  pl.BlockSpec((nb, c, mix), tok),
                  pl.BlockSpec((nb, 8, width3), per_b3),
                  pl.BlockSpec((8, width3), const2),
                  pl.BlockSpec((8, LANES), const2),
                  pl.BlockSpec((nb, N_HEADS, HEAD_DIM, HEAD_DIM), per_b4),
                  pl.BlockSpec((1, HEAD_DIM), const2)],
        out_specs=(pl.BlockSpec((nb, c, mix), tok),
                   pl.BlockSpec((nb, N_HEADS, HEAD_DIM, HEAD_DIM), per_b4),
                   pl.BlockSpec((nb, 8, width3), per_b3)),
        scratch_shapes=[pltpu.VMEM((nb, N_HEADS, HEAD_DIM, HEAD_DIM), F32),
                        pltpu.VMEM((nb, c + 8, width3), F32)],
        compiler_params=_cparams(("parallel", "arbitrary")),
        name="delta",
    )(pre, bd, dg, conv_buf, conv_w, hp, s0, gain)
    return o_d, s_new, conv_out[:, 5:8, :]


def _post_kernel(x_ref, oa_ref, od_ref, wmg_ref, wbr_ref, wo_ref, lng_ref, lnb_ref, y_ref, *, d, alpha):
    x = x_ref[...]
    gates = _sigmoid(_dot(x.astype(BF16), wmg_ref[...]))
    h_a = _dot(oa_ref[...].astype(BF16), wbr_ref[0])
    h_d = _dot(od_ref[...].astype(BF16), wbr_ref[1])
    merged = gates[:, :d] * h_a + gates[:, d:] * h_d
    y = _dot(merged.astype(BF16), wo_ref[...])
    z = alpha * x + y
    mu = jnp.mean(z, axis=-1, keepdims=True)
    zc = z - mu
    var = jnp.mean(zc * zc, axis=-1, keepdims=True)
    y_ref[...] = zc * lax.rsqrt(var + LN_EPS) * lng_ref[...] + lnb_ref[...]


def _post(x, o_a, o_d, w_mg, w_br, w_o, ln_g, ln_b, *, alpha):
    m, d = x.shape
    tm = min(512, m)
    row = lambda i: (i, 0)
    c2 = lambda i: (0, 0)
    return pl.pallas_call(
        functools.partial(_post_kernel, d=d, alpha=alpha),
        out_shape=jax.ShapeDtypeStruct((m, d), F32),
        grid=(m // tm,),
        in_specs=[pl.BlockSpec((tm, d), row), pl.BlockSpec((tm, d // 2), row), pl.BlockSpec((tm, d // 2), row),
                  pl.BlockSpec(w_mg.shape, c2), pl.BlockSpec(w_br.shape, lambda i: (0, 0, 0)),
                  pl.BlockSpec(w_o.shape, c2), pl.BlockSpec((1, d), c2), pl.BlockSpec((1, d), c2)],
        out_specs=pl.BlockSpec((tm, d), row),
        compiler_params=_cparams(("parallel",)),
        name="post",
    )(x, o_a, o_d, w_mg, w_br, w_o, ln_g, ln_b)


def kernel(x_prompt, x_sample, cache_k, cache_v, state_delta, state_conv, page_table, rel_bias_table,
           w_in, conv_w, lambda_q1, lambda_k1, lambda_q2, lambda_k2, diff_norm_g, a_log, dt_bias,
           delta_norm_g, w_branch, w_o, ln_g, ln_b):
    batch, seq, d = x_prompt.shape
    dec_b, dec_t, _ = x_sample.shape
    depth = w_in.shape[0]
    mix = d // 2
    assert mix == N_HEADS * HEAD_DIM and dec_t == 4 and cache_k.shape[2] == PAGE
    alpha = (2 * depth) ** 0.25
    t_attn = min(512, seq)
    c_prompt = min(64, seq)
    c_sample = 8
    nb_prompt = math.gcd(batch, 2)
    nb_sample = math.gcd(dec_b, 2)
    n_pool = cache_k.shape[1]

    bias_p, bias_sp, bias_sn, lam = _prep(rel_bias_table, lambda_q1, lambda_k1, lambda_q2, lambda_k2,
                                          t_attn, depth)
    cache_k2 = cache_k.reshape(depth, n_pool, PAGE * N_HEADS, HEAD_DIM)
    cache_v2 = cache_v.reshape(depth, n_pool, PAGE * N_HEADS, HEAD_DIM)

    hp = x_prompt.reshape(batch * seq, d)
    hs = x_sample.reshape(dec_b * dec_t, d)
    outs = {k: [] for k in ("ks", "vs", "dp", "ds", "cp", "cs")}
    k_all = jnp.zeros((depth, batch * seq * N_HEADS, HEAD_DIM), F32)
    v_all = jnp.zeros((depth, batch * seq * N_HEADS, HEAD_DIM), F32)
    zeros_buf = jnp.zeros((batch, 8, 3 * mix), F32)
    zeros_state = jnp.zeros((batch, N_HEADS, HEAD_DIM, HEAD_DIM), F32)

    for l in range(depth):
        w_main = w_in[l, :, :8 * mix].astype(BF16)
        w_bd = jnp.pad(w_in[l, :, 8 * mix:8 * mix + 2 * N_HEADS], ((0, 0), (0, LANES - 2 * N_HEADS))).astype(BF16)
        w_mg = w_in[l, :, 8 * mix + 2 * N_HEADS:].astype(BF16)
        w_t = jnp.stack([w_in[l, :, :mix].T, w_in[l, :, 2 * mix:3 * mix].T]).astype(BF16)
        w_br = w_branch[l].astype(BF16)
        w_ol = w_o[l].astype(BF16)
        cw = jnp.pad(conv_w[l], ((0, 8 - CONV_W), (0, 0)))
        hpar = jnp.zeros((8, LANES), F32)
        hpar = hpar.at[0, N_HEADS:2 * N_HEADS].set(a_log[l]).at[1, N_HEADS:2 * N_HEADS].set(dt_bias[l])
        gain_a = diff_norm_g[l][None, :]
        gain_d = delta_norm_g[l][None, :]
        lng = ln_g[l][None, :]
        lnb = ln_b[l][None, :]

        qt, k_all, kb, v_all, vt, ag, pre, dg, bd = _inproj(hp, w_main, w_bd, w_t, (k_all, v_all),
                                                            tm=t_attn, layer=l, depth=depth)
        o_a = _attn_prompt(qt, kb, vt, bias_p, ag, lam, gain_a, batch=batch, t=t_attn, layer=l)
        o_d, s_new, c_new = _delta(pre.reshape(batch, seq, 3 * mix), bd.reshape(batch, seq, LANES),
                                   dg.reshape(batch, seq, mix), zeros_buf, cw, hpar, zeros_state, gain_d,
                                   nb=nb_prompt, c=c_prompt, t_valid=seq)
        hp = _post(hp, o_a, o_d.reshape(batch * seq, mix), w_mg, w_br, w_ol, lng, lnb, alpha=alpha)
        outs["dp"].append(s_new)
        outs["cp"].append(c_new)

        q_s, k_s, v_s, ag_s, pre_s, dg_s, bd_s = _inproj(hs, w_main, w_bd, tm=dec_b * dec_t)
        o_a = _attn_sample(q_s.reshape(dec_b, dec_t, mix), k_s.reshape(dec_b, dec_t, mix),
                           v_s.reshape(dec_b, dec_t, mix), cache_k2, cache_v2, page_table,
                           bias_sp, bias_sn, ag_s.reshape(dec_b, dec_t, mix), lam, gain_a, layer=l)

        def pad_rows(a):
            a = a.reshape(dec_b, dec_t, a.shape[-1])
            return jnp.pad(a, ((0, 0), (0, c_sample - dec_t), (0, 0)))

        buf_s = jnp.pad(state_conv[l], ((0, 0), (8 - (CONV_W - 1), 0), (0, 0)))
        o_d, s_new, c_new = _delta(pad_rows(pre_s), pad_rows(bd_s), pad_rows(dg_s), buf_s, cw, hpar,
                                   state_delta[l], gain_d, nb=nb_sample, c=c_sample, t_valid=dec_t)
        o_d = o_d[:, :dec_t].reshape(dec_b * dec_t, mix)
        hs = _post(hs, o_a.reshape(dec_b * dec_t, mix), o_d, w_mg, w_br, w_ol, lng, lnb, alpha=alpha)
        outs["ks"].append(k_s.reshape(dec_b, dec_t, N_HEADS, HEAD_DIM))
        outs["vs"].append(v_s.reshape(dec_b, dec_t, N_HEADS, HEAD_DIM))
        outs["ds"].append(s_new)
        outs["cs"].append(c_new)

    st = {k: jnp.stack(v) for k, v in outs.items()}
    kv_shape = (depth, batch, seq, N_HEADS, HEAD_DIM)
    return (hp.reshape(batch, seq, d), hs.reshape(dec_b, dec_t, d), k_all.reshape(kv_shape), v_all.reshape(kv_shape),
            st["ks"], st["vs"], st["dp"], st["ds"], st["cp"], st["cs"])
```

```python
import functools
import math

import jax
import jax.numpy as jnp
import numpy as np
from jax import lax
from jax.experimental import pallas as pl
from jax.experimental.pallas import tpu as pltpu

F32 = jnp.float32
BF16 = jnp.bfloat16

N_HEADS = 4
HEAD_DIM = 128
QK_DIM = 64
CONV_W = 4
PAGE = 128
N_BUCKETS = 32
MAX_DISTANCE = 128
LN_EPS = 1e-5
NORM_EPS = 1e-6
NEG_INF = -1e30
LOG2E = math.log2(math.e)

LANES = 128
SUBLANES = 8
VMEM_LIMIT = 56 * 1024 * 1024

NN = (((1,), (0,)), ((), ()))
NT = (((1,), (1,)), ((), ()))


def _dot(a, b, dims=NN):
    return lax.dot_general(a, b, dims, preferred_element_type=F32)


def _split(a):
    hi = a.astype(BF16)
    lo = (a - hi.astype(F32)).astype(BF16)
    return hi, lo


def _dot3s(a_parts, b_parts, dims=NN):
    ah, al = a_parts
    bh, bl = b_parts
    return _dot(ah, bh, dims) + (_dot(ah, bl, dims) + _dot(al, bh, dims))


def _dot2(a_exact, b, dims=NN):
    bh, bl = _split(b)
    a = a_exact.astype(BF16)
    return _dot(a, bh, dims) + _dot(a, bl, dims)


def _silu(x):
    return x * (1.0 / (1.0 + jnp.exp(-x)))


def _sigmoid(x):
    return 1.0 / (1.0 + jnp.exp(-x))


def _softplus(x):
    return jnp.maximum(x, 0.0) + jnp.log(1.0 + jnp.exp(-jnp.abs(x)))


def _cparams(sem):
    return pltpu.CompilerParams(dimension_semantics=sem, vmem_limit_bytes=VMEM_LIMIT)


def _bucket_np(dist):
    dist = np.asarray(dist, np.int64)
    n = np.maximum(dist, 0)
    max_exact = N_BUCKETS // 2
    nf = np.maximum(n, 1).astype(np.float32)
    large = max_exact + (np.log(nf / np.float32(max_exact)) / np.float32(math.log(MAX_DISTANCE / max_exact))
                         * np.float32(N_BUCKETS - max_exact)).astype(np.int32)
    large = np.minimum(large, N_BUCKETS - 1)
    b = np.where(n < max_exact, n, large)
    return np.where(dist < 0, -1, b).astype(np.int32)


def _prep_kernel(table_ref, bkt_p_ref, bkt_sp_ref, bkt_sn_ref, lq1_ref, lk1_ref, lq2_ref, lk2_ref,
                 bias_p_ref, bias_sp_ref, bias_sn_ref, lam_ref, *, depth):
    def lookup(bkt, h, unit):
        far = table_ref[N_BUCKETS - 1, h]
        out = jnp.full(bkt.shape, NEG_INF, F32)
        for b in range(N_BUCKETS):
            out = jnp.where(bkt == b, (table_ref[b, h] - far) * unit, out)
        return out

    for h in range(N_HEADS):
        for kind in range(2):
            bias_p_ref[h, kind] = lookup(bkt_p_ref[kind], h, LOG2E)
        bias_sp_ref[h * 8:(h + 1) * 8, :] = lookup(bkt_sp_ref[...], h, 1.0)
        bias_sn_ref[h * 8:(h + 1) * 8, :] = lookup(bkt_sn_ref[...], h, 1.0)

    rows = lax.broadcasted_iota(jnp.int32, (SUBLANES, LANES), 0)
    lam = jnp.zeros((SUBLANES, LANES), F32)
    for l in range(depth):
        lam_init = 0.8 - 0.6 * math.exp(-0.3 * l)
        s1 = jnp.sum(lq1_ref[l:l + 1, :] * lk1_ref[l:l + 1, :], axis=1, keepdims=True)
        s2 = jnp.sum(lq2_ref[l:l + 1, :] * lk2_ref[l:l + 1, :], axis=1, keepdims=True)
        val = jnp.exp(s1) - jnp.exp(s2) + lam_init
        lam = jnp.where(rows == l, val, lam)
    lam_ref[...] = lam


def _prep(table, lq1, lk1, lq2, lk2, t, depth):
    qry = np.arange(t)[:, None]
    key = np.arange(t)[None, :]
    bkt_p = np.stack([_bucket_np(qry - key), _bucket_np(t + qry - key)])
    tq = np.tile(np.arange(4), 2)[:, None]
    bkt_sp = _bucket_np(PAGE + tq - np.arange(PAGE)[None, :])
    cn = np.arange(LANES)[None, :]
    bkt_sn = np.where(cn < 4, _bucket_np(tq - cn), -1).astype(np.int32)
    vm = pl.BlockSpec(memory_space=pltpu.VMEM)
    return pl.pallas_call(
        functools.partial(_prep_kernel, depth=depth),
        out_shape=(jax.ShapeDtypeStruct((N_HEADS, 2, t, t), F32),
                   jax.ShapeDtypeStruct((N_HEADS * 8, PAGE), F32),
                   jax.ShapeDtypeStruct((N_HEADS * 8, LANES), F32),
                   jax.ShapeDtypeStruct((SUBLANES, LANES), F32)),
        in_specs=[pl.BlockSpec(memory_space=pltpu.SMEM), vm, vm, vm, vm, vm, vm, vm],
        out_specs=(vm, vm, vm, vm),
        compiler_params=pltpu.CompilerParams(vmem_limit_bytes=VMEM_LIMIT),
        name="prep",
    )(table, jnp.asarray(bkt_p), jnp.asarray(bkt_sp), jnp.asarray(bkt_sn), lq1, lk1, lq2, lk2)


def _inproj_kernel(x_ref, w_ref, wbd_ref, *refs, mix, tm, prompt, n_prev):
    if prompt:
        wt_ref = refs[0]
        qs_ref, k_ref, kt_ref, v_ref, vb_ref, ag_ref, pre_ref, dg_ref, bd_ref = refs[1 + n_prev:]
    else:
        q_ref, k_ref, v_ref, ag_ref, pre_ref, dg_ref, bd_ref = refs
    xb = x_ref[...].astype(BF16)

    def seg(i, n=1):
        return _dot(xb, w_ref[:, i * mix:(i + n) * mix])

    scale = QK_DIM ** -0.5
    k = seg(1)
    v = seg(2)
    if prompt:
        q = seg(0) * (scale * LOG2E)
        lane = lax.broadcasted_iota(jnp.int32, q.shape, 1)
        first = (lane % HEAD_DIM) < QK_DIM
        qs_ref[0] = jnp.where(first, q, 0.0).astype(BF16)
        qs_ref[1] = jnp.where(first, 0.0, q).astype(BF16)
        kt_ref[...] = _dot(wt_ref[...], xb, NT).astype(BF16)
        vb_ref[...] = v.astype(BF16)
        for h in range(N_HEADS):
            k_ref[pl.ds(h, tm, stride=N_HEADS), :] = k[:, h * HEAD_DIM:(h + 1) * HEAD_DIM]
            v_ref[pl.ds(h, tm, stride=N_HEADS), :] = v[:, h * HEAD_DIM:(h + 1) * HEAD_DIM]
    else:
        q_ref[...] = seg(0) * scale
        k_ref[...] = k
        v_ref[...] = v
    ag_ref[...] = _silu(seg(3)).astype(ag_ref.dtype)
    pre_ref[...] = seg(4, 3)
    dg_ref[...] = _silu(seg(7)).astype(dg_ref.dtype)
    bd_ref[...] = _dot(xb, wbd_ref[...])


def _inproj(x, w_main, w_bd, w_t=None, kv_prev=None, *, tm, layer=0, depth=1):
    m, d = x.shape
    mix = d // 2
    prompt = w_t is not None
    row = lambda i: (i, 0)
    col = lambda i: (0, i)
    full = lambda i: (0, 0)
    o_mix = pl.BlockSpec((tm, mix), row)
    f_mix = jax.ShapeDtypeStruct((m, mix), F32)
    b_mix = jax.ShapeDtypeStruct((m, mix), BF16)
    in_specs = [pl.BlockSpec((tm, d), row), pl.BlockSpec(w_main.shape, full), pl.BlockSpec(w_bd.shape, full)]
    args = [x, w_main, w_bd]
    aliases = {}
    n_prev = 0
    if prompt:
        in_specs.append(pl.BlockSpec(w_t.shape, full))
        args.append(w_t)
        if kv_prev is not None:
            n_prev = 2
            in_specs += [pl.BlockSpec(memory_space=pl.ANY)] * 2
            args += list(kv_prev)
            aliases = {4: 1, 5: 3}
        kv_all = jax.ShapeDtypeStruct((depth, m * N_HEADS, HEAD_DIM), F32)
        kv_spec = pl.BlockSpec((None, tm * N_HEADS, HEAD_DIM), lambda i: (layer, i, 0))
        shapes = [jax.ShapeDtypeStruct((2, m, mix), BF16), kv_all, jax.ShapeDtypeStruct((mix, m), BF16),
                  kv_all, b_mix]
        specs = [pl.BlockSpec((2, tm, mix), lambda i: (0, i, 0)), kv_spec, pl.BlockSpec((mix, tm), col),
                 kv_spec, o_mix]
    else:
        shapes = [f_mix] * 3
        specs = [o_mix] * 3
    gate = b_mix if prompt else f_mix
    shapes += [gate, jax.ShapeDtypeStruct((m, 3 * mix), F32), gate, jax.ShapeDtypeStruct((m, LANES), F32)]
    specs += [o_mix, pl.BlockSpec((tm, 3 * mix), row), o_mix, pl.BlockSpec((tm, LANES), row)]
    return pl.pallas_call(
        functools.partial(_inproj_kernel, mix=mix, tm=tm, prompt=prompt, n_prev=n_prev),
        out_shape=tuple(shapes),
        grid=(m // tm,),
        in_specs=in_specs,
        out_specs=tuple(specs),
        input_output_aliases=aliases,
        compiler_params=_cparams(("parallel",)),
        name="inproj",
    )(*args)


def _head_norm_gate(o, gain, gate, scale):
    ms = jnp.mean(o * o, axis=-1, keepdims=True)
    return o * lax.rsqrt(ms + NORM_EPS) * (gain * scale) * gate


def _attn_prompt_kernel(ii_ref, jj_ref, qs_ref, kt_ref, vb_ref, bias_ref, ag_ref, lam_ref, gain_ref,
                        o_ref, m_ref, acc_ref, *, t, rw, layer, out_scale):
    s_idx = pl.program_id(1)
    i = ii_ref[s_idx]
    j = jj_ref[s_idx]
    units = [(h, c) for h in range(N_HEADS) for c in range(2 * t // rw)]

    @pl.when(j == 0)
    def _():
        m_ref[...] = jnp.full(m_ref.shape, NEG_INF, F32)
        acc_ref[...] = jnp.zeros(acc_ref.shape, F32)

    def scores(u, kind):
        h, c = u
        r0 = c * rw
        cols = slice(h * HEAD_DIM, (h + 1) * HEAD_DIM)
        q = qs_ref[r0 // t, r0 % t:r0 % t + rw, cols]
        s = _dot(q, kt_ref[cols, :])
        if kind is not None:
            s = s + bias_ref[h, kind, r0 % t:r0 % t + rw, :]
        return s

    def update(kind):
        ones = jnp.ones((t, LANES), BF16)
        v_aug = [jnp.concatenate([vb_ref[:, h * HEAD_DIM:(h + 1) * HEAD_DIM], ones], axis=1)
                 for h in range(N_HEADS)]
        s_next = scores(units[0], kind)
        for n, u in enumerate(units):
            h, c = u
            s = s_next
            if n + 1 < len(units):
                s_next = scores(units[n + 1], kind)
            rows = slice(c * rw, (c + 1) * rw)
            m_prev = m_ref[h, rows, :]
            m_new = jnp.maximum(m_prev, jnp.max(s, axis=1, keepdims=True))
            alpha = jnp.exp2(m_prev - m_new)
            p = jnp.exp2(s - jnp.concatenate([m_new] * (t // LANES), axis=1)).astype(BF16)
            acc_ref[h, rows, :] = (jnp.concatenate([alpha, alpha], axis=1) * acc_ref[h, rows, :]
                                   + _dot(p, v_aug[h]))
            m_ref[h, rows, :] = m_new

    @pl.when(j < i - 1)
    def _():
        update(None)

    @pl.when(j == i - 1)
    def _():
        update(1)

    @pl.when(j == i)
    def _():
        update(0)
        lam = lam_ref[layer:layer + 1, :]
        for h in range(N_HEADS):
            cols = slice(h * HEAD_DIM, (h + 1) * HEAD_DIM)
            o = acc_ref[h, :, :HEAD_DIM] / acc_ref[h, :, HEAD_DIM:]
            od = o[:t] - lam * o[t:]
            o_ref[:, cols] = _head_norm_gate(od, gain_ref[...], ag_ref[:, cols].astype(F32),
                                             out_scale).astype(o_ref.dtype)


def _attn_prompt(qs, kt, vb, bias_p, ag, lam, gain, *, batch, t, layer):
    m, mix = vb.shape
    n_t = m // batch // t
    ii = np.concatenate([np.full(i + 1, i) for i in range(n_t)]).astype(np.int32)
    jj = np.concatenate([np.arange(i + 1) for i in range(n_t)]).astype(np.int32)
    lam_init = 0.8 - 0.6 * math.exp(-0.3 * layer)
    rw = min(t, 2 * LANES)
    q_rows = lambda b, s, ii, jj: (b * n_t + ii[s], 0)
    k_rows = lambda b, s, ii, jj: (b * n_t + jj[s], 0)
    k_cols = lambda b, s, ii, jj: (0, b * n_t + jj[s])
    const2 = lambda b, s, ii, jj: (0, 0)
    grid_spec = pltpu.PrefetchScalarGridSpec(
        num_scalar_prefetch=2,
        grid=(batch, len(ii)),
        in_specs=[pl.BlockSpec((2, t, mix), lambda b, s, ii, jj: (0, b * n_t + ii[s], 0)),
                  pl.BlockSpec((mix, t), k_cols),
                  pl.BlockSpec((t, mix), k_rows),
                  pl.BlockSpec(memory_space=pltpu.VMEM),
                  pl.BlockSpec((t, mix), q_rows),
                  pl.BlockSpec((SUBLANES, LANES), const2),
                  pl.BlockSpec((1, HEAD_DIM), const2)],
        out_specs=pl.BlockSpec((t, mix), q_rows),
        scratch_shapes=[pltpu.VMEM((N_HEADS, 2 * t, LANES), F32),
                        pltpu.VMEM((N_HEADS, 2 * t, HEAD_DIM + LANES), F32)],
    )
    return pl.pallas_call(
        functools.partial(_attn_prompt_kernel, t=t, rw=rw, layer=layer, out_scale=1.0 - lam_init),
        out_shape=jax.ShapeDtypeStruct((m, mix), BF16),
        grid_spec=grid_spec,
        compiler_params=_cparams(("parallel", "arbitrary")),
        name="attn_prompt",
    )(jnp.asarray(ii), jnp.asarray(jj), qs, kt, vb, bias_p, ag, lam, gain)


def _attn_sample_kernel(*refs, g, layer, out_scale):
    pt_ref = refs[0]
    q_ref, kn_ref, vn_ref = refs[1:4]
    k_refs = refs[4:4 + g]
    v_refs = refs[4 + g:4 + 2 * g]
    bias_sp_ref, bias_sn_ref, ag_ref, lam_ref, gain_ref = refs[4 + 2 * g:9 + 2 * g]
    o_ref = refs[9 + 2 * g]
    wq_ref, m_ref, l_ref, acc_ref, own_ref = refs[10 + 2 * g:]
    del pt_ref
    step = pl.program_id(1)
    n_steps = pl.num_programs(1)
    rows = N_HEADS * 8
    width = N_HEADS * HEAD_DIM

    @pl.when(step == 0)
    def _():
        r = lax.broadcasted_iota(jnp.int32, (rows, width), 0)
        c = lax.broadcasted_iota(jnp.int32, (rows, width), 1)
        keep = (c // HEAD_DIM == r // 8) & ((c % HEAD_DIM) // QK_DIM == (r // 4) % 2)
        wq = jnp.zeros((rows, width), F32)
        for tq in range(4):
            wq = jnp.where(keep & (r % 4 == tq), q_ref[tq:tq + 1, :], wq)
        wq_ref[...] = wq.astype(BF16)
        m_ref[...] = jnp.full(m_ref.shape, NEG_INF, F32)
        l_ref[...] = jnp.zeros(l_ref.shape, F32)
        acc_ref[...] = jnp.zeros(acc_ref.shape, F32)
        own_ref[...] = jnp.zeros(own_ref.shape, F32)

    def update(s, v_list):
        m_prev = m_ref[...]
        m_new = jnp.maximum(m_prev, jnp.max(s, axis=1, keepdims=True))
        alpha = jnp.exp(m_prev - m_new)
        p = jnp.exp(s - m_new)
        l_ref[...] = alpha * l_ref[...] + jnp.sum(p, axis=1, keepdims=True)
        p = p.astype(BF16)
        pv = None
        for idx, vb in enumerate(v_list):
            w = vb.shape[0]
            part = _dot(p[:, idx * PAGE:idx * PAGE + w] if w == PAGE else p[:, :w], vb)
            pv = part if pv is None else pv + part
        acc_ref[...] = alpha * acc_ref[...] + pv
        m_ref[...] = m_new

    wq = wq_ref[...]

    def page(ref):
        return jnp.concatenate([ref[pl.ds(h, PAGE, stride=N_HEADS), :] for h in range(N_HEADS)],
                               axis=1).astype(BF16)

    def page_scores():
        return [_dot(wq, page(k_refs[idx]), NT) for idx in range(g)]

    @pl.when(step < n_steps - 1)
    def _():
        update(jnp.concatenate(page_scores(), axis=1), [page(v_refs[idx]) for idx in range(g)])

    @pl.when(step == n_steps - 1)
    def _():
        sc = page_scores()
        sc[g - 1] = sc[g - 1] + bias_sp_ref[...]
        update(jnp.concatenate(sc, axis=1), [page(v_refs[idx]) for idx in range(g)])
        own_ref[0, 0:4, :] = kn_ref[...]
        own_ref[1, 0:4, :] = vn_ref[...]
        update(_dot(wq, own_ref[0].astype(BF16), NT) + bias_sn_ref[...], [own_ref[1].astype(BF16)])
        o = acc_ref[...] / l_ref[...]
        lam = lam_ref[layer:layer + 1, :]
        for h in range(N_HEADS):
            blk = o[h * 8:(h + 1) * 8, h * HEAD_DIM:(h + 1) * HEAD_DIM]
            od = (blk - lam * pltpu.roll(blk, 4, axis=0))[0:4]
            gate = ag_ref[:, h * HEAD_DIM:(h + 1) * HEAD_DIM].astype(F32)
            o_ref[:, h * HEAD_DIM:(h + 1) * HEAD_DIM] = _head_norm_gate(
                od, gain_ref[...], gate, out_scale).astype(o_ref.dtype)


def _attn_sample(q, k_new, v_new, cache_k, cache_v, page_table, bias_sp, bias_sn, ag, lam, gain, *, layer):
    bsz, tq, width = q.shape
    n_pages = page_table.shape[1]
    g = math.gcd(n_pages, 16)
    lam_init = 0.8 - 0.6 * math.exp(-0.3 * layer)
    per_b = lambda b, s, pt: (b, 0, 0)
    const2 = lambda b, s, pt: (0, 0)

    def page_spec(idx):
        return pl.BlockSpec((None, None, PAGE * N_HEADS, HEAD_DIM),
                            lambda b, s, pt: (layer, pt[b, s * g + idx], 0, 0))

    grid_spec = pltpu.PrefetchScalarGridSpec(
        num_scalar_prefetch=1,
        grid=(bsz, n_pages // g),
        in_specs=([pl.BlockSpec((None, tq, width), per_b)] * 3
                  + [page_spec(idx) for idx in range(g)] * 2
                  + [pl.BlockSpec(bias_sp.shape, const2), pl.BlockSpec(bias_sn.shape, const2),
                     pl.BlockSpec((None, tq, width), per_b),
                     pl.BlockSpec((SUBLANES, LANES), const2), pl.BlockSpec((1, HEAD_DIM), const2)]),
        out_specs=pl.BlockSpec((None, tq, width), per_b),
        scratch_shapes=[pltpu.VMEM((N_HEADS * 8, width), BF16), pltpu.VMEM((N_HEADS * 8, 1), F32),
                        pltpu.VMEM((N_HEADS * 8, 1), F32), pltpu.VMEM((N_HEADS * 8, width), F32),
                        pltpu.VMEM((2, PAGE, width), F32)],
    )
    return pl.pallas_call(
        functools.partial(_attn_sample_kernel, g=g, layer=layer, out_scale=1.0 - lam_init),
        out_shape=jax.ShapeDtypeStruct((bsz, tq, width), F32),
        grid_spec=grid_spec,
        compiler_params=_cparams(("parallel", "arbitrary")),
        name="attn_sample",
    )(page_table, q, k_new, v_new, *([cache_k] * g), *([cache_v] * g), bias_sp, bias_sn, ag, lam, gain)


def _unit_lower_inverses(lmats, c):
    r = lax.broadcasted_iota(jnp.int32, (c, c), 0)
    q = lax.broadcasted_iota(jnp.int32, (c, c), 1)
    eye = (r == q).astype(F32)
    base = min(c, 16)
    same_base = r // base == q // base
    zero = jnp.zeros((c, c), BF16)

    def masked(parts, mask, sign):
        return tuple(jnp.where(mask, sign * part, zero) for part in parts)

    l_parts = [_split(lm) for lm in lmats]
    p_parts = [masked(lp, same_base, -1) for lp in l_parts]
    xs = [eye + jnp.where(same_base, -lm, 0.0) for lm in lmats]
    for _ in range(int(math.log2(base)) - 1):
        p_parts = [_split(_dot3s(pp, pp)) for pp in p_parts]
        xs = [x + _dot3s(_split(x), pp) for x, pp in zip(xs, p_parts)]
    x_parts = [_split(x) for x in xs]
    s = base
    while s < c:
        below = (r // (2 * s) == q // (2 * s)) & (r // s != q // s)
        t_parts = [_split(_dot3s(masked(lp, below, 1), xp)) for lp, xp in zip(l_parts, x_parts)]
        xs = [x - _dot3s(xp, tp) for x, xp, tp in zip(xs, x_parts, t_parts)]
        x_parts = [_split(x) for x in xs]
        s *= 2
    return x_parts


def _delta_kernel(pre_ref, bd_ref, dg_ref, buf_ref, cw_ref, hp_ref, s0_ref, gain_ref,
                  o_ref, s_out_ref, conv_out_ref, s_ref, xcat_ref, *, nb, c, mix, t_valid):
    n = pl.program_id(1)
    n_chunks = pl.num_programs(1)
    units = [(b, h) for b in range(nb) for h in range(N_HEADS)]

    @pl.when(n == 0)
    def _():
        s_ref[...] = s0_ref[...]
        xcat_ref[:, 0:8, :] = buf_ref[...]

    r = lax.broadcasted_iota(jnp.int32, (c, c), 0)
    q_i = lax.broadcasted_iota(jnp.int32, (c, c), 1)
    incl = r >= q_i
    strict = r > q_i
    incl_f = incl.astype(F32)
    sel = (lax.broadcasted_iota(jnp.int32, (8, LANES), 1)
           == lax.broadcasted_iota(jnp.int32, (8, LANES), 0) + 4).astype(F32)
    eye_h = (lax.broadcasted_iota(jnp.int32, (HEAD_DIM, HEAD_DIM), 0)
             == lax.broadcasted_iota(jnp.int32, (HEAD_DIM, HEAD_DIM), 1)).astype(BF16)
    last_valid = t_valid - (t_valid - 1) // c * c

    ys, beta_ts, gcum_ts, gcum_rows = [], [], [], []
    for b in range(nb):
        xcat_ref[b, 8:8 + c, :] = pre_ref[b]
        y = xcat_ref[b, 5:5 + c, :] * cw_ref[0:1, :]
        for j in range(1, CONV_W):
            y = y + xcat_ref[b, 5 + j:5 + j + c, :] * cw_ref[j:j + 1, :]
        ys.append(_silu(y))
        conv_out_ref[b] = xcat_ref[b, last_valid:last_valid + 8, :]
        xcat_ref[b, 0:8, :] = xcat_ref[b, c:c + 8, :]

        bd = bd_ref[b]
        beta_t = _sigmoid(bd)
        g_t = -jnp.exp(hp_ref[0:1, :]) * _softplus(bd + hp_ref[1:2, :])
        if t_valid % c:
            row_ok = (lax.broadcasted_iota(jnp.int32, bd.shape, 0) + n * c) < t_valid
            beta_t = jnp.where(row_ok, beta_t, 0.0)
            g_t = jnp.where(row_ok, g_t, 0.0)
        beta_ts.append(beta_t)
        gcum_ts.append(_dot2(incl_f, g_t))
    for b in range(nb):
        gcum_rows.append(_dot2(sel, gcum_ts[b], NT))

    qn, kn, vb_, kb, e_g, decay, g_last, gcum = {}, {}, {}, {}, {}, {}, {}, {}
    for u in units:
        b, h = u
        lo, hi = h * HEAD_DIM, (h + 1) * HEAD_DIM
        qh = ys[b][:, lo:hi]
        kh = ys[b][:, mix + lo:mix + hi]
        qn[u] = qh * lax.rsqrt(jnp.sum(qh * qh, axis=-1, keepdims=True) + NORM_EPS) * (HEAD_DIM ** -0.5)
        kn[u] = kh * lax.rsqrt(jnp.sum(kh * kh, axis=-1, keepdims=True) + NORM_EPS)
        beta = beta_ts[b][:, h:h + 1]
        gcum[u] = gcum_ts[b][:, 4 + h:5 + h]
        g_last[u] = gcum_ts[b][c - 1:c, 4 + h:5 + h]
        diff = gcum[u] - gcum_rows[b][h:h + 1, :]
        decay[u] = jnp.where(incl, jnp.exp(jnp.where(incl, diff, 0.0)), 0.0)
        kb[u] = kn[u] * beta
        vb_[u] = ys[b][:, 2 * mix + lo:2 * mix + hi] * beta
        e_g[u] = jnp.exp(gcum[u])
    knb = {u: kn[u].astype(BF16) for u in units}
    lmat = {u: jnp.where(strict, _dot(kb[u].astype(BF16), knb[u], NT) * decay[u], 0.0) for u in units}
    a_qk = {u: jnp.where(incl, _dot(qn[u].astype(BF16), knb[u], NT) * decay[u], 0.0).astype(BF16)
            for u in units}

    tinv = dict(zip(units, _unit_lower_inverses([lmat[u] for u in units], c)))
    uw = {u: _dot3s(tinv[u], _split(jnp.concatenate([vb_[u], kb[u] * e_g[u]], axis=1))) for u in units}

    s_old = {u: s_ref[u[0], u[1]] for u in units}
    s_b = {u: s_old[u].astype(BF16) for u in units}
    ws = {u: _dot(uw[u][:, HEAD_DIM:].astype(BF16), s_b[u]) for u in units}
    qs = {u: _dot((qn[u] * e_g[u]).astype(BF16), s_b[u]) for u in units}
    k_tail_t = {u: _dot(eye_h, (kn[u] * jnp.exp(g_last[u] - gcum[u])).astype(BF16), NT).astype(BF16)
                for u in units}
    vnb = {u: (uw[u][:, :HEAD_DIM] - ws[u]).astype(BF16) for u in units}
    o = {u: qs[u] + _dot(a_qk[u], vnb[u]) for u in units}
    for u in units:
        s_ref[u[0], u[1]] = s_old[u] * jnp.exp(g_last[u]) + _dot(k_tail_t[u], vnb[u])
    for u in units:
        b, h = u
        lo, hi = h * HEAD_DIM, (h + 1) * HEAD_DIM
        ms = jnp.mean(o[u] * o[u], axis=-1, keepdims=True)
        o_n = o[u] * lax.rsqrt(ms + NORM_EPS) * gain_ref[...]
        o_ref[b, :, lo:hi] = (o_n * dg_ref[b, :, lo:hi].astype(F32)).astype(o_ref.dtype)

    @pl.when(n == n_chunks - 1)
    def _():
        s_out_ref[...] = s_ref[...]


def _delta(pre, bd, dg, conv_buf, conv_w, hp, s0, gain, *, nb, c, t_valid):
    batch, t_pad, width3 = pre.shape
    mix = width3 // 3
    n_chunks = t_pad // c
    tok = lambda b, n: (b, n, 0)
    per_b3 = lambda b, n: (b, 0, 0)
    per_b4 = lambda b, n: (b, 0, 0, 0)
    const2 = lambda b, n: (0, 0)
    o_d, s_new, conv_out = pl.pallas_call(
        functools.partial(_delta_kernel, nb=nb, c=c, mix=mix, t_valid=t_valid),
        out_shape=(jax.ShapeDtypeStruct((batch, t_pad, mix), dg.dtype),
                   jax.ShapeDtypeStruct(s0.shape, F32),
                   jax.ShapeDtypeStruct((batch, 8, width3), F32)),
        grid=(batch // nb, n_chunks),
        in_specs=[pl.BlockSpec((nb, c, width3), tok), pl.BlockSpec((nb, c, LANES), tok),
                  pl.BlockSpec((nb, c, mix), tok),
                  pl.BlockSpec((nb, 8, width3), per_b3),
                  pl.BlockSpec((8, width3), const2),
                  pl.BlockSpec((8, LANES), const2),
                  pl.BlockSpec((nb, N_HEADS, HEAD_DIM, HEAD_DIM), per_b4),
                  pl.BlockSpec((1, HEAD_DIM), const2)],
        out_specs=(pl.BlockSpec((nb, c, mix), tok),
                   pl.BlockSpec((nb, N_HEADS, HEAD_DIM, HEAD_DIM), per_b4),
                   pl.BlockSpec((nb, 8, width3), per_b3)),
        scratch_shapes=[pltpu.VMEM((nb, N_HEADS, HEAD_DIM, HEAD_DIM), F32),
                        pltpu.VMEM((nb, c + 8, width3), F32)],
        compiler_params=_cparams(("parallel", "arbitrary")),
        name="delta",
    )(pre, bd, dg, conv_buf, conv_w, hp, s0, gain)
    return o_d, s_new, conv_out[:, 5:8, :]


def _post_kernel(x_ref, oa_ref, od_ref, wmg_ref, wbr_ref, wo_ref, lng_ref, lnb_ref, y_ref, *, d, alpha):
    x = x_ref[...]
    gates = _sigmoid(_dot(x.astype(BF16), wmg_ref[...]))
    h_a = _dot(oa_ref[...].astype(BF16), wbr_ref[0])
    h_d = _dot(od_ref[...].astype(BF16), wbr_ref[1])
    merged = gates[:, :d] * h_a + gates[:, d:] * h_d
    y = _dot(merged.astype(BF16), wo_ref[...])
    z = alpha * x + y
    mu = jnp.mean(z, axis=-1, keepdims=True)
    zc = z - mu
    var = jnp.mean(zc * zc, axis=-1, keepdims=True)
    y_ref[...] = zc * lax.rsqrt(var + LN_EPS) * lng_ref[...] + lnb_ref[...]


def _post(x, o_a, o_d, w_mg, w_br, w_o, ln_g, ln_b, *, alpha):
    m, d = x.shape
    tm = min(512, m)
    row = lambda i: (i, 0)
    c2 = lambda i: (0, 0)
    return pl.pallas_call(
        functools.partial(_post_kernel, d=d, alpha=alpha),
        out_shape=jax.ShapeDtypeStruct((m, d), F32),
        grid=(m // tm,),
        in_specs=[pl.BlockSpec((tm, d), row), pl.BlockSpec((tm, d // 2), row), pl.BlockSpec((tm, d // 2), row),
                  pl.BlockSpec(w_mg.shape, c2), pl.BlockSpec(w_br.shape, lambda i: (0, 0, 0)),
                  pl.BlockSpec(w_o.shape, c2), pl.BlockSpec((1, d), c2), pl.BlockSpec((1, d), c2)],
        out_specs=pl.BlockSpec((tm, d), row),
        compiler_params=_cparams(("parallel",)),
        name="post",
    )(x, o_a, o_d, w_mg, w_br, w_o, ln_g, ln_b)


def kernel(x_prompt, x_sample, cache_k, cache_v, state_delta, state_conv, page_table, rel_bias_table,
           w_in, conv_w, lambda_q1, lambda_k1, lambda_q2, lambda_k2, diff_norm_g, a_log, dt_bias,
           delta_norm_g, w_branch, w_o, ln_g, ln_b):
    batch, seq, d = x_prompt.shape
    dec_b, dec_t, _ = x_sample.shape
    depth = w_in.shape[0]
    mix = d // 2
    assert mix == N_HEADS * HEAD_DIM and dec_t == 4 and cache_k.shape[2] == PAGE
    alpha = (2 * depth) ** 0.25
    t_attn = min(512, seq)
    c_prompt = min(64, seq)
    c_sample = 8
    nb_prompt = math.gcd(batch, 2)
    nb_sample = math.gcd(dec_b, 2)
    n_pool = cache_k.shape[1]

    bias_p, bias_sp, bias_sn, lam = _prep(rel_bias_table, lambda_q1, lambda_k1, lambda_q2, lambda_k2,
                                          t_attn, depth)
    cache_k2 = cache_k.reshape(depth, n_pool, PAGE * N_HEADS, HEAD_DIM)
    cache_v2 = cache_v.reshape(depth, n_pool, PAGE * N_HEADS, HEAD_DIM)

    hp = x_prompt.reshape(batch * seq, d)
    hs = x_sample.reshape(dec_b * dec_t, d)
    outs = {k: [] for k in ("ks", "vs", "dp", "ds", "cp", "cs")}
    k_all = jnp.zeros((depth, batch * seq * N_HEADS, HEAD_DIM), F32)
    v_all = jnp.zeros((depth, batch * seq * N_HEADS, HEAD_DIM), F32)
    zeros_buf = jnp.zeros((batch, 8, 3 * mix), F32)
    zeros_state = jnp.zeros((batch, N_HEADS, HEAD_DIM, HEAD_DIM), F32)

    for l in range(depth):
        w_main = w_in[l, :, :8 * mix].astype(BF16)
        w_bd = jnp.pad(w_in[l, :, 8 * mix:8 * mix + 2 * N_HEADS], ((0, 0), (0, LANES - 2 * N_HEADS))).astype(BF16)
        w_mg = w_in[l, :, 8 * mix + 2 * N_HEADS:].astype(BF16)
        w_kt = w_in[l, :, mix:2 * mix].T.astype(BF16)
        w_br = w_branch[l].astype(BF16)
        w_ol = w_o[l].astype(BF16)
        cw = jnp.pad(conv_w[l], ((0, 8 - CONV_W), (0, 0)))
        hpar = jnp.zeros((8, LANES), F32)
        hpar = hpar.at[0, N_HEADS:2 * N_HEADS].set(a_log[l]).at[1, N_HEADS:2 * N_HEADS].set(dt_bias[l])
        gain_a = diff_norm_g[l][None, :]
        gain_d = delta_norm_g[l][None, :]
        lng = ln_g[l][None, :]
        lnb = ln_b[l][None, :]

        qs, k_all, kt, v_all, vb, ag, pre, dg, bd = _inproj(hp, w_main, w_bd, w_kt, (k_all, v_all),
                                                            tm=t_attn, layer=l, depth=depth)
        o_a = _attn_prompt(qs, kt, vb, bias_p, ag, lam, gain_a, batch=batch, t=t_attn, layer=l)
        o_d, s_new, c_new = _delta(pre.reshape(batch, seq, 3 * mix), bd.reshape(batch, seq, LANES),
                                   dg.reshape(batch, seq, mix), zeros_buf, cw, hpar, zeros_state, gain_d,
                                   nb=nb_prompt, c=c_prompt, t_valid=seq)
        hp = _post(hp, o_a, o_d.reshape(batch * seq, mix), w_mg, w_br, w_ol, lng, lnb, alpha=alpha)
        outs["dp"].append(s_new)
        outs["cp"].append(c_new)

        q_s, k_s, v_s, ag_s, pre_s, dg_s, bd_s = _inproj(hs, w_main, w_bd, tm=dec_b * dec_t)
        o_a = _attn_sample(q_s.reshape(dec_b, dec_t, mix), k_s.reshape(dec_b, dec_t, mix),
                           v_s.reshape(dec_b, dec_t, mix), cache_k2, cache_v2, page_table,
                           bias_sp, bias_sn, ag_s.reshape(dec_b, dec_t, mix), lam, gain_a, layer=l)

        def pad_rows(a):
            a = a.reshape(dec_b, dec_t, a.shape[-1])
            return jnp.pad(a, ((0, 0), (0, c_sample - dec_t), (0, 0)))

        buf_s = jnp.pad(state_conv[l], ((0, 0), (8 - (CONV_W - 1), 0), (0, 0)))
        o_d, s_new, c_new = _delta(pad_rows(pre_s), pad_rows(bd_s), pad_rows(dg_s), buf_s, cw, hpar,
                                   state_delta[l], gain_d, nb=nb_sample, c=c_sample, t_valid=dec_t)
        o_d = o_d[:, :dec_t].reshape(dec_b * dec_t, mix)
        hs = _post(hs, o_a.reshape(dec_b * dec_t, mix), o_d, w_mg, w_br, w_ol, lng, lnb, alpha=alpha)
        outs["ks"].append(k_s.reshape(dec_b, dec_t, N_HEADS, HEAD_DIM))
        outs["vs"].append(v_s.reshape(dec_b, dec_t, N_HEADS, HEAD_DIM))
        outs["ds"].append(s_new)
        outs["cs"].append(c_new)

    st = {k: jnp.stack(v) for k, v in outs.items()}
    kv_shape = (depth, batch, seq, N_HEADS, HEAD_DIM)
    return (hp.reshape(batch, seq, d), hs.reshape(dec_b, dec_t, d), k_all.reshape(kv_shape), v_all.reshape(kv_shape),
            st["ks"], st["vs"], st["dp"], st["ds"], st["cp"], st["cs"])
```

```python
import functools
import math

import jax
import jax.numpy as jnp
import numpy as np
from jax import lax
from jax.experimental import pallas as pl
from jax.experimental.pallas import tpu as pltpu

F32 = jnp.float32
BF16 = jnp.bfloat16

N_HEADS = 4
HEAD_DIM = 128
QK_DIM = 64
CONV_W = 4
PAGE = 128
N_BUCKETS = 32
MAX_DISTANCE = 128
LN_EPS = 1e-5
NORM_EPS = 1e-6
NEG_INF = -1e30
LOG2E = math.log2(math.e)

LANES = 128
SUBLANES = 8
VMEM_LIMIT = 56 * 1024 * 1024

NN = (((1,), (0,)), ((), ()))
NT = (((1,), (1,)), ((), ()))


def _dot(a, b, dims=NN):
    return lax.dot_general(a, b, dims, preferred_element_type=F32)


def _split(a):
    hi = a.astype(BF16)
    lo = (a - hi.astype(F32)).astype(BF16)
    return hi, lo


def _dot3s(a_parts, b_parts, dims=NN):
    ah, al = a_parts
    bh, bl = b_parts
    return _dot(ah, bh, dims) + (_dot(ah, bl, dims) + _dot(al, bh, dims))


def _dot2(a_exact, b, dims=NN):
    bh, bl = _split(b)
    a = a_exact.astype(BF16)
    return _dot(a, bh, dims) + _dot(a, bl, dims)


def _silu(x):
    return x * (1.0 / (1.0 + jnp.exp(-x)))


def _sigmoid(x):
    return 1.0 / (1.0 + jnp.exp(-x))


def _softplus(x):
    return jnp.maximum(x, 0.0) + jnp.log(1.0 + jnp.exp(-jnp.abs(x)))


def _cparams(sem):
    return pltpu.CompilerParams(dimension_semantics=sem, vmem_limit_bytes=VMEM_LIMIT)


def _bucket_np(dist):
    dist = np.asarray(dist, np.int64)
    n = np.maximum(dist, 0)
    max_exact = N_BUCKETS // 2
    nf = np.maximum(n, 1).astype(np.float32)
    large = max_exact + (np.log(nf / np.float32(max_exact)) / np.float32(math.log(MAX_DISTANCE / max_exact))
                         * np.float32(N_BUCKETS - max_exact)).astype(np.int32)
    large = np.minimum(large, N_BUCKETS - 1)
    b = np.where(n < max_exact, n, large)
    return np.where(dist < 0, -1, b).astype(np.int32)


def _prep_kernel(table_ref, bkt_p_ref, bkt_sp_ref, bkt_sn_ref, lq1_ref, lk1_ref, lq2_ref, lk2_ref,
                 bias_p_ref, bias_sp_ref, bias_sn_ref, lam_ref, *, depth):
    def lookup(bkt, h, unit):
        far = table_ref[N_BUCKETS - 1, h]
        out = jnp.full(bkt.shape, NEG_INF, F32)
        for b in range(N_BUCKETS):
            out = jnp.where(bkt == b, (table_ref[b, h] - far) * unit, out)
        return out

    for h in range(N_HEADS):
        for kind in range(2):
            bias_p_ref[h, kind] = lookup(bkt_p_ref[kind], h, LOG2E)
        bias_sp_ref[h * 8:(h + 1) * 8, :] = lookup(bkt_sp_ref[...], h, 1.0)
        bias_sn_ref[h * 8:(h + 1) * 8, :] = lookup(bkt_sn_ref[...], h, 1.0)

    rows = lax.broadcasted_iota(jnp.int32, (SUBLANES, LANES), 0)
    lam = jnp.zeros((SUBLANES, LANES), F32)
    for l in range(depth):
        lam_init = 0.8 - 0.6 * math.exp(-0.3 * l)
        s1 = jnp.sum(lq1_ref[l:l + 1, :] * lk1_ref[l:l + 1, :], axis=1, keepdims=True)
        s2 = jnp.sum(lq2_ref[l:l + 1, :] * lk2_ref[l:l + 1, :], axis=1, keepdims=True)
        val = jnp.exp(s1) - jnp.exp(s2) + lam_init
        lam = jnp.where(rows == l, val, lam)
    lam_ref[...] = lam


def _prep(table, lq1, lk1, lq2, lk2, t, depth):
    qry = np.arange(t)[:, None]
    key = np.arange(t)[None, :]
    bkt_p = np.stack([_bucket_np(qry - key), _bucket_np(t + qry - key)])
    tq = np.tile(np.arange(4), 2)[:, None]
    bkt_sp = _bucket_np(PAGE + tq - np.arange(PAGE)[None, :])
    cn = np.arange(LANES)[None, :]
    bkt_sn = np.where(cn < 4, _bucket_np(tq - cn), -1).astype(np.int32)
    vm = pl.BlockSpec(memory_space=pltpu.VMEM)
    return pl.pallas_call(
        functools.partial(_prep_kernel, depth=depth),
        out_shape=(jax.ShapeDtypeStruct((N_HEADS, 2, t, t), F32),
                   jax.ShapeDtypeStruct((N_HEADS * 8, PAGE), F32),
                   jax.ShapeDtypeStruct((N_HEADS * 8, LANES), F32),
                   jax.ShapeDtypeStruct((SUBLANES, LANES), F32)),
        in_specs=[pl.BlockSpec(memory_space=pltpu.SMEM), vm, vm, vm, vm, vm, vm, vm],
        out_specs=(vm, vm, vm, vm),
        compiler_params=pltpu.CompilerParams(vmem_limit_bytes=VMEM_LIMIT),
        name="prep",
    )(table, jnp.asarray(bkt_p), jnp.asarray(bkt_sp), jnp.asarray(bkt_sn), lq1, lk1, lq2, lk2)


def _inproj_kernel(x_ref, w_ref, wbd_ref, *refs, mix, tm, prompt, n_prev):
    if prompt:
        wt_ref = refs[0]
        qs_ref, k_ref, kt_ref, v_ref, vb_ref, ag_ref, pre_ref, dg_ref, bd_ref = refs[1 + n_prev:]
    else:
        q_ref, k_ref, v_ref, ag_ref, pre_ref, dg_ref, bd_ref = refs
    xb = x_ref[...].astype(BF16)

    def seg(i, n=1):
        return _dot(xb, w_ref[:, i * mix:(i + n) * mix])

    scale = QK_DIM ** -0.5
    k = seg(1)
    v = seg(2)
    if prompt:
        q = seg(0) * (scale * LOG2E)
        lane = lax.broadcasted_iota(jnp.int32, q.shape, 1)
        first = (lane % HEAD_DIM) < QK_DIM
        qs_ref[0] = jnp.where(first, q, 0.0).astype(BF16)
        qs_ref[1] = jnp.where(first, 0.0, q).astype(BF16)
        kt_ref[...] = _dot(wt_ref[...], xb, NT).astype(BF16)
        vb_ref[...] = v.astype(BF16)
        for h in range(N_HEADS):
            k_ref[pl.ds(h, tm, stride=N_HEADS), :] = k[:, h * HEAD_DIM:(h + 1) * HEAD_DIM]
            v_ref[pl.ds(h, tm, stride=N_HEADS), :] = v[:, h * HEAD_DIM:(h + 1) * HEAD_DIM]
    else:
        q_ref[...] = seg(0) * scale
        k_ref[...] = k
        v_ref[...] = v
    ag_ref[...] = _silu(seg(3)).astype(ag_ref.dtype)
    pre_ref[...] = seg(4, 3)
    dg_ref[...] = _silu(seg(7)).astype(dg_ref.dtype)
    bd_ref[...] = _dot(xb, wbd_ref[...])


def _inproj(x, w_main, w_bd, w_t=None, kv_prev=None, *, tm, layer=0, depth=1):
    m, d = x.shape
    mix = d // 2
    prompt = w_t is not None
    row = lambda i: (i, 0)
    col = lambda i: (0, i)
    full = lambda i: (0, 0)
    o_mix = pl.BlockSpec((tm, mix), row)
    f_mix = jax.ShapeDtypeStruct((m, mix), F32)
    b_mix = jax.ShapeDtypeStruct((m, mix), BF16)
    in_specs = [pl.BlockSpec((tm, d), row), pl.BlockSpec(w_main.shape, full), pl.BlockSpec(w_bd.shape, full)]
    args = [x, w_main, w_bd]
    aliases = {}
    n_prev = 0
    if prompt:
        in_specs.append(pl.BlockSpec(w_t.shape, full))
        args.append(w_t)
        if kv_prev is not None:
            n_prev = 2
            in_specs += [pl.BlockSpec(memory_space=pl.ANY)] * 2
            args += list(kv_prev)
            aliases = {4: 1, 5: 3}
        kv_all = jax.ShapeDtypeStruct((depth, m * N_HEADS, HEAD_DIM), F32)
        kv_spec = pl.BlockSpec((None, tm * N_HEADS, HEAD_DIM), lambda i: (layer, i, 0))
        shapes = [jax.ShapeDtypeStruct((2, m, mix), BF16), kv_all, jax.ShapeDtypeStruct((mix, m), BF16),
                  kv_all, b_mix]
        specs = [pl.BlockSpec((2, tm, mix), lambda i: (0, i, 0)), kv_spec, pl.BlockSpec((mix, tm), col),
                 kv_spec, o_mix]
    else:
        shapes = [f_mix] * 3
        specs = [o_mix] * 3
    gate = b_mix if prompt else f_mix
    shapes += [gate, jax.ShapeDtypeStruct((m, 3 * mix), F32), gate, jax.ShapeDtypeStruct((m, LANES), F32)]
    specs += [o_mix, pl.BlockSpec((tm, 3 * mix), row), o_mix, pl.BlockSpec((tm, LANES), row)]
    return pl.pallas_call(
        functools.partial(_inproj_kernel, mix=mix, tm=tm, prompt=prompt, n_prev=n_prev),
        out_shape=tuple(shapes),
        grid=(m // tm,),
        in_specs=in_specs,
        out_specs=tuple(specs),
        input_output_aliases=aliases,
        compiler_params=_cparams(("parallel",)),
        name="inproj",
    )(*args)


def _head_norm_gate(o, gain, gate, scale):
    ms = jnp.mean(o * o, axis=-1, keepdims=True)
    return o * lax.rsqrt(ms + NORM_EPS) * (gain * scale) * gate


def _attn_prompt_kernel(ii_ref, jj_ref, qs_ref, kt_ref, vb_ref, bias_ref, ag_ref, lam_ref, gain_ref,
                        o_ref, m_ref, acc_ref, *, t, rw, layer, out_scale):
    s_idx = pl.program_id(1)
    i = ii_ref[s_idx]
    j = jj_ref[s_idx]
    units = [(h, c) for h in range(N_HEADS) for c in range(2 * t // rw)]

    @pl.when(j == 0)
    def _():
        m_ref[...] = jnp.full(m_ref.shape, NEG_INF, F32)
        acc_ref[...] = jnp.zeros(acc_ref.shape, F32)

    def scores(u, kind):
        h, c = u
        r0 = c * rw
        cols = slice(h * HEAD_DIM, (h + 1) * HEAD_DIM)
        q = qs_ref[r0 // t, r0 % t:r0 % t + rw, cols]
        s = _dot(q, kt_ref[cols, :])
        if kind is not None:
            s = s + bias_ref[h, kind, r0 % t:r0 % t + rw, :]
        return s

    def update(kind):
        ones = jnp.ones((t, LANES), BF16)
        v_aug = [jnp.concatenate([vb_ref[:, h * HEAD_DIM:(h + 1) * HEAD_DIM], ones], axis=1)
                 for h in range(N_HEADS)]
        s_next = scores(units[0], kind)
        for n, u in enumerate(units):
            h, c = u
            s = s_next
            if n + 1 < len(units):
                s_next = scores(units[n + 1], kind)
            rows = slice(c * rw, (c + 1) * rw)
            m_prev = m_ref[h, rows, :]
            m_new = jnp.maximum(m_prev, jnp.max(s, axis=1, keepdims=True))
            alpha = jnp.exp2(m_prev - m_new)
            p = jnp.exp2(s - jnp.concatenate([m_new] * (t // LANES), axis=1)).astype(BF16)
            acc_ref[h, rows, :] = (jnp.concatenate([alpha, alpha], axis=1) * acc_ref[h, rows, :]
                                   + _dot(p, v_aug[h]))
            m_ref[h, rows, :] = m_new

    @pl.when(j < i - 1)
    def _():
        update(None)

    @pl.when(j == i - 1)
    def _():
        update(1)

    @pl.when(j == i)
    def _():
        update(0)
        lam = lam_ref[layer:layer + 1, :]
        for h in range(N_HEADS):
            cols = slice(h * HEAD_DIM, (h + 1) * HEAD_DIM)
            o = acc_ref[h, :, :HEAD_DIM] / acc_ref[h, :, HEAD_DIM:]
            od = o[:t] - lam * o[t:]
            o_ref[:, cols] = _head_norm_gate(od, gain_ref[...], ag_ref[:, cols].astype(F32),
                                             out_scale).astype(o_ref.dtype)


def _attn_prompt(qs, kt, vb, bias_p, ag, lam, gain, *, batch, t, layer):
    m, mix = vb.shape
    n_t = m // batch // t
    ii = np.concatenate([np.full(i + 1, i) for i in range(n_t)]).astype(np.int32)
    jj = np.concatenate([np.arange(i + 1) for i in range(n_t)]).astype(np.int32)
    lam_init = 0.8 - 0.6 * math.exp(-0.3 * layer)
    rw = min(t, 2 * LANES)
    q_rows = lambda b, s, ii, jj: (b * n_t + ii[s], 0)
    k_rows = lambda b, s, ii, jj: (b * n_t + jj[s], 0)
    k_cols = lambda b, s, ii, jj: (0, b * n_t + jj[s])
    const2 = lambda b, s, ii, jj: (0, 0)
    grid_spec = pltpu.PrefetchScalarGridSpec(
        num_scalar_prefetch=2,
        grid=(batch, len(ii)),
        in_specs=[pl.BlockSpec((2, t, mix), lambda b, s, ii, jj: (0, b * n_t + ii[s], 0)),
                  pl.BlockSpec((mix, t), k_cols),
                  pl.BlockSpec((t, mix), k_rows),
                  pl.BlockSpec(memory_space=pltpu.VMEM),
                  pl.BlockSpec((t, mix), q_rows),
                  pl.BlockSpec((SUBLANES, LANES), const2),
                  pl.BlockSpec((1, HEAD_DIM), const2)],
        out_specs=pl.BlockSpec((t, mix), q_rows),
        scratch_shapes=[pltpu.VMEM((N_HEADS, 2 * t, LANES), F32),
                        pltpu.VMEM((N_HEADS, 2 * t, HEAD_DIM + LANES), F32)],
    )
    return pl.pallas_call(
        functools.partial(_attn_prompt_kernel, t=t, rw=rw, layer=layer, out_scale=1.0 - lam_init),
        out_shape=jax.ShapeDtypeStruct((m, mix), BF16),
        grid_spec=grid_spec,
        compiler_params=_cparams(("parallel", "arbitrary")),
        name="attn_prompt",
    )(jnp.asarray(ii), jnp.asarray(jj), qs, kt, vb, bias_p, ag, lam, gain)


def _attn_sample_kernel(*refs, g, layer, out_scale):
    pt_ref = refs[0]
    q_ref, kn_ref, vn_ref = refs[1:4]
    k_refs = refs[4:4 + g]
    v_refs = refs[4 + g:4 + 2 * g]
    bias_sp_ref, bias_sn_ref, ag_ref, lam_ref, gain_ref = refs[4 + 2 * g:9 + 2 * g]
    o_ref = refs[9 + 2 * g]
    wq_ref, m_ref, l_ref, acc_ref, own_ref = refs[10 + 2 * g:]
    del pt_ref
    step = pl.program_id(1)
    n_steps = pl.num_programs(1)
    rows = N_HEADS * 8
    width = N_HEADS * HEAD_DIM

    @pl.when(step == 0)
    def _():
        r = lax.broadcasted_iota(jnp.int32, (rows, width), 0)
        c = lax.broadcasted_iota(jnp.int32, (rows, width), 1)
        keep = (c // HEAD_DIM == r // 8) & ((c % HEAD_DIM) // QK_DIM == (r // 4) % 2)
        wq = jnp.zeros((rows, width), F32)
        for tq in range(4):
            wq = jnp.where(keep & (r % 4 == tq), q_ref[tq:tq + 1, :], wq)
        wq_ref[...] = wq.astype(BF16)
        m_ref[...] = jnp.full(m_ref.shape, NEG_INF, F32)
        l_ref[...] = jnp.zeros(l_ref.shape, F32)
        acc_ref[...] = jnp.zeros(acc_ref.shape, F32)
        own_ref[...] = jnp.zeros(own_ref.shape, F32)

    def update(s, v_list):
        m_prev = m_ref[...]
        m_new = jnp.maximum(m_prev, jnp.max(s, axis=1, keepdims=True))
        alpha = jnp.exp(m_prev - m_new)
        p = jnp.exp(s - m_new)
        l_ref[...] = alpha * l_ref[...] + jnp.sum(p, axis=1, keepdims=True)
        p = p.astype(BF16)
        pv = None
        for idx, vb in enumerate(v_list):
            w = vb.shape[0]
            part = _dot(p[:, idx * PAGE:idx * PAGE + w] if w == PAGE else p[:, :w], vb)
            pv = part if pv is None else pv + part
        acc_ref[...] = alpha * acc_ref[...] + pv
        m_ref[...] = m_new

    wq = wq_ref[...]

    def page(ref):
        return jnp.concatenate([ref[pl.ds(h, PAGE, stride=N_HEADS), :] for h in range(N_HEADS)],
                               axis=1).astype(BF16)

    def page_scores():
        return [_dot(wq, page(k_refs[idx]), NT) for idx in range(g)]

    @pl.when(step < n_steps - 1)
    def _():
        update(jnp.concatenate(page_scores(), axis=1), [page(v_refs[idx]) for idx in range(g)])

    @pl.when(step == n_steps - 1)
    def _():
        sc = page_scores()
        sc[g - 1] = sc[g - 1] + bias_sp_ref[...]
        update(jnp.concatenate(sc, axis=1), [page(v_refs[idx]) for idx in range(g)])
        own_ref[0, 0:4, :] = kn_ref[...]
        own_ref[1, 0:4, :] = vn_ref[...]
        update(_dot(wq, own_ref[0].astype(BF16), NT) + bias_sn_ref[...], [own_ref[1].astype(BF16)])
        o = acc_ref[...] / l_ref[...]
        lam = lam_ref[layer:layer + 1, :]
        for h in range(N_HEADS):
            blk = o[h * 8:(h + 1) * 8, h * HEAD_DIM:(h + 1) * HEAD_DIM]
            od = (blk - lam * pltpu.roll(blk, 4, axis=0))[0:4]
            gate = ag_ref[:, h * HEAD_DIM:(h + 1) * HEAD_DIM].astype(F32)
            o_ref[:, h * HEAD_DIM:(h + 1) * HEAD_DIM] = _head_norm_gate(
                od, gain_ref[...], gate, out_scale).astype(o_ref.dtype)


def _attn_sample(q, k_new, v_new, cache_k, cache_v, page_table, bias_sp, bias_sn, ag, lam, gain, *, layer):
    bsz, tq, width = q.shape
    n_pages = page_table.shape[1]
    g = math.gcd(n_pages, 32)
    lam_init = 0.8 - 0.6 * math.exp(-0.3 * layer)
    per_b = lambda b, s, pt: (b, 0, 0)
    const2 = lambda b, s, pt: (0, 0)

    def page_spec(idx):
        return pl.BlockSpec((None, None, PAGE * N_HEADS, HEAD_DIM),
                            lambda b, s, pt: (layer, pt[b, s * g + idx], 0, 0))

    grid_spec = pltpu.PrefetchScalarGridSpec(
        num_scalar_prefetch=1,
        grid=(bsz, n_pages // g),
        in_specs=([pl.BlockSpec((None, tq, width), per_b)] * 3
                  + [page_spec(idx) for idx in range(g)] * 2
                  + [pl.BlockSpec(bias_sp.shape, const2), pl.BlockSpec(bias_sn.shape, const2),
                     pl.BlockSpec((None, tq, width), per_b),
                     pl.BlockSpec((SUBLANES, LANES), const2), pl.BlockSpec((1, HEAD_DIM), const2)]),
        out_specs=pl.BlockSpec((None, tq, width), per_b),
        scratch_shapes=[pltpu.VMEM((N_HEADS * 8, width), BF16), pltpu.VMEM((N_HEADS * 8, 1), F32),
                        pltpu.VMEM((N_HEADS * 8, 1), F32), pltpu.VMEM((N_HEADS * 8, width), F32),
                        pltpu.VMEM((2, PAGE, width), F32)],
    )
    return pl.pallas_call(
        functools.partial(_attn_sample_kernel, g=g, layer=layer, out_scale=1.0 - lam_init),
        out_shape=jax.ShapeDtypeStruct((bsz, tq, width), F32),
        grid_spec=grid_spec,
        compiler_params=_cparams(("parallel", "arbitrary")),
        name="attn_sample",
    )(page_table, q, k_new, v_new, *([cache_k] * g), *([cache_v] * g), bias_sp, bias_sn, ag, lam, gain)


def _unit_lower_inverses(lmats, c):
    r = lax.broadcasted_iota(jnp.int32, (c, c), 0)
    q = lax.broadcasted_iota(jnp.int32, (c, c), 1)
    eye = (r == q).astype(F32)
    base = min(c, 16)
    same_base = r // base == q // base
    zero = jnp.zeros((c, c), BF16)

    def masked(parts, mask, sign):
        return tuple(jnp.where(mask, sign * part, zero) for part in parts)

    l_parts = [_split(lm) for lm in lmats]
    p_parts = [masked(lp, same_base, -1) for lp in l_parts]
    xs = [eye + jnp.where(same_base, -lm, 0.0) for lm in lmats]
    for _ in range(int(math.log2(base)) - 1):
        p_parts = [_split(_dot3s(pp, pp)) for pp in p_parts]
        xs = [x + _dot3s(_split(x), pp) for x, pp in zip(xs, p_parts)]
    x_parts = [_split(x) for x in xs]
    s = base
    while s < c:
        below = (r // (2 * s) == q // (2 * s)) & (r // s != q // s)
        t_parts = [_split(_dot3s(masked(lp, below, 1), xp)) for lp, xp in zip(l_parts, x_parts)]
        xs = [x - _dot3s(xp, tp) for x, xp, tp in zip(xs, x_parts, t_parts)]
        x_parts = [_split(x) for x in xs]
        s *= 2
    return x_parts


def _delta_kernel(pre_ref, bd_ref, dg_ref, buf_ref, cw_ref, hp_ref, s0_ref, gain_ref,
                  o_ref, s_out_ref, conv_out_ref, s_ref, xcat_ref, *, nb, c, mix, t_valid):
    n = pl.program_id(1)
    n_chunks = pl.num_programs(1)
    units = [(b, h) for b in range(nb) for h in range(N_HEADS)]

    @pl.when(n == 0)
    def _():
        s_ref[...] = s0_ref[...]
        xcat_ref[:, 0:8, :] = buf_ref[...]

    r = lax.broadcasted_iota(jnp.int32, (c, c), 0)
    q_i = lax.broadcasted_iota(jnp.int32, (c, c), 1)
    incl = r >= q_i
    strict = r > q_i
    incl_f = incl.astype(F32)
    sel = (lax.broadcasted_iota(jnp.int32, (8, LANES), 1)
           == lax.broadcasted_iota(jnp.int32, (8, LANES), 0) + 4).astype(F32)
    eye_h = (lax.broadcasted_iota(jnp.int32, (HEAD_DIM, HEAD_DIM), 0)
             == lax.broadcasted_iota(jnp.int32, (HEAD_DIM, HEAD_DIM), 1)).astype(BF16)
    last_valid = t_valid - (t_valid - 1) // c * c

    ys, beta_ts, gcum_ts, gcum_rows = [], [], [], []
    for b in range(nb):
        xcat_ref[b, 8:8 + c, :] = pre_ref[b]
        y = xcat_ref[b, 5:5 + c, :] * cw_ref[0:1, :]
        for j in range(1, CONV_W):
            y = y + xcat_ref[b, 5 + j:5 + j + c, :] * cw_ref[j:j + 1, :]
        ys.append(_silu(y))
        conv_out_ref[b] = xcat_ref[b, last_valid:last_valid + 8, :]
        xcat_ref[b, 0:8, :] = xcat_ref[b, c:c + 8, :]

        bd = bd_ref[b]
        beta_t = _sigmoid(bd)
        g_t = -jnp.exp(hp_ref[0:1, :]) * _softplus(bd + hp_ref[1:2, :])
        if t_valid % c:
            row_ok = (lax.broadcasted_iota(jnp.int32, bd.shape, 0) + n * c) < t_valid
            beta_t = jnp.where(row_ok, beta_t, 0.0)
            g_t = jnp.where(row_ok, g_t, 0.0)
        beta_ts.append(beta_t)
        gcum_ts.append(_dot2(incl_f, g_t))
    for b in range(nb):
        gcum_rows.append(_dot2(sel, gcum_ts[b], NT))

    qn, kn, vb_, kb, e_g, decay, g_last, gcum = {}, {}, {}, {}, {}, {}, {}, {}
    for u in units:
        b, h = u
        lo, hi = h * HEAD_DIM, (h + 1) * HEAD_DIM
        qh = ys[b][:, lo:hi]
        kh = ys[b][:, mix + lo:mix + hi]
        qn[u] = qh * lax.rsqrt(jnp.sum(qh * qh, axis=-1, keepdims=True) + NORM_EPS) * (HEAD_DIM ** -0.5)
        kn[u] = kh * lax.rsqrt(jnp.sum(kh * kh, axis=-1, keepdims=True) + NORM_EPS)
        beta = beta_ts[b][:, h:h + 1]
        gcum[u] = gcum_ts[b][:, 4 + h:5 + h]
        g_last[u] = gcum_ts[b][c - 1:c, 4 + h:5 + h]
        diff = gcum[u] - gcum_rows[b][h:h + 1, :]
        decay[u] = jnp.where(incl, jnp.exp(jnp.where(incl, diff, 0.0)), 0.0)
        kb[u] = kn[u] * beta
        vb_[u] = ys[b][:, 2 * mix + lo:2 * mix + hi] * beta
        e_g[u] = jnp.exp(gcum[u])
    knb = {u: kn[u].astype(BF16) for u in units}
    lmat = {u: jnp.where(strict, _dot(kb[u].astype(BF16), knb[u], NT) * decay[u], 0.0) for u in units}
    a_qk = {u: jnp.where(incl, _dot(qn[u].astype(BF16), knb[u], NT) * decay[u], 0.0).astype(BF16)
            for u in units}

    tinv = dict(zip(units, _unit_lower_inverses([lmat[u] for u in units], c)))
    uw = {u: _dot3s(tinv[u], _split(jnp.concatenate([vb_[u], kb[u] * e_g[u]], axis=1))) for u in units}

    s_old = {u: s_ref[u[0], u[1]] for u in units}
    s_b = {u: s_old[u].astype(BF16) for u in units}
    ws = {u: _dot(uw[u][:, HEAD_DIM:].astype(BF16), s_b[u]) for u in units}
    qs = {u: _dot((qn[u] * e_g[u]).astype(BF16), s_b[u]) for u in units}
    k_tail_t = {u: _dot(eye_h, (kn[u] * jnp.exp(g_last[u] - gcum[u])).astype(BF16), NT).astype(BF16)
                for u in units}
    vnb = {u: (uw[u][:, :HEAD_DIM] - ws[u]).astype(BF16) for u in units}
    o = {u: qs[u] + _dot(a_qk[u], vnb[u]) for u in units}
    for u in units:
        s_ref[u[0], u[1]] = s_old[u] * jnp.exp(g_last[u]) + _dot(k_tail_t[u], vnb[u])
    for u in units:
        b, h = u
        lo, hi = h * HEAD_DIM, (h + 1) * HEAD_DIM
        ms = jnp.mean(o[u] * o[u], axis=-1, keepdims=True)
        o_n = o[u] * lax.rsqrt(ms + NORM_EPS) * gain_ref[...]
        o_ref[b, :, lo:hi] = (o_n * dg_ref[b, :, lo:hi].astype(F32)).astype(o_ref.dtype)

    @pl.when(n == n_chunks - 1)
    def _():
        s_out_ref[...] = s_ref[...]


def _delta(pre, bd, dg, conv_buf, conv_w, hp, s0, gain, *, nb, c, t_valid):
    batch, t_pad, width3 = pre.shape
    mix = width3 // 3
    n_chunks = t_pad // c
    tok = lambda b, n: (b, n, 0)
    per_b3 = lambda b, n: (b, 0, 0)
    per_b4 = lambda b, n: (b, 0, 0, 0)
    const2 = lambda b, n: (0, 0)
    o_d, s_new, conv_out = pl.pallas_call(
        functools.partial(_delta_kernel, nb=nb, c=c, mix=mix, t_valid=t_valid),
        out_shape=(jax.ShapeDtypeStruct((batch, t_pad, mix), dg.dtype),
                   jax.ShapeDtypeStruct(s0.shape, F32),
                   jax.ShapeDtypeStruct((batch, 8, width3), F32)),
        grid=(batch // nb, n_chunks),
        in_specs=[pl.BlockSpec((nb, c, width3), tok), pl.BlockSpec((nb, c, LANES), tok),
                  pl.BlockSpec((nb, c, mix), tok),
                  pl.BlockSpec((nb, 8, width3), per_b3),
                  pl.BlockSpec((8, width3), const2),
                  pl.BlockSpec((8, LANES), const2),
                  pl.BlockSpec((nb, N_HEADS, HEAD_DIM, HEAD_DIM), per_b4),
                  pl.BlockSpec((1, HEAD_DIM), const2)],
        out_specs=(pl.BlockSpec((nb, c, mix), tok),
                   pl.BlockSpec((nb, N_HEADS, HEAD_DIM, HEAD_DIM), per_b4),
                   pl.BlockSpec((nb, 8, width3), per_b3)),
        scratch_shapes=[pltpu.VMEM((nb, N_HEADS, HEAD_DIM, HEAD_DIM), F32),
                        pltpu.VMEM((nb, c + 8, width3), F32)],
        compiler_params=_cparams(("parallel", "arbitrary")),
        name="delta",
    )(pre, bd, dg, conv_buf, conv_w, hp, s0, gain)
    return o_d, s_new, conv_out[:, 5:8, :]


def _post_kernel(x_ref, oa_ref, od_ref, wmg_ref, wbr_ref, wo_ref, lng_ref, lnb_ref, y_ref, *, d, alpha):
    x = x_ref[...]
    gates = _sigmoid(_dot(x.astype(BF16), wmg_ref[...]))
    h_a = _dot(oa_ref[...].astype(BF16), wbr_ref[0])
    h_d = _dot(od_ref[...].astype(BF16), wbr_ref[1])
    merged = gates[:, :d] * h_a + gates[:, d:] * h_d
    y = _dot(merged.astype(BF16), wo_ref[...])
    z = alpha * x + y
    mu = jnp.mean(z, axis=-1, keepdims=True)
    zc = z - mu
    var = jnp.mean(zc * zc, axis=-1, keepdims=True)
    y_ref[...] = zc * lax.rsqrt(var + LN_EPS) * lng_ref[...] + lnb_ref[...]


def _post(x, o_a, o_d, w_mg, w_br, w_o, ln_g, ln_b, *, alpha):
    m, d = x.shape
    tm = min(512, m)
    row = lambda i: (i, 0)
    c2 = lambda i: (0, 0)
    return pl.pallas_call(
        functools.partial(_post_kernel, d=d, alpha=alpha),
        out_shape=jax.ShapeDtypeStruct((m, d), F32),
        grid=(m // tm,),
        in_specs=[pl.BlockSpec((tm, d), row), pl.BlockSpec((tm, d // 2), row), pl.BlockSpec((tm, d // 2), row),
                  pl.BlockSpec(w_mg.shape, c2), pl.BlockSpec(w_br.shape, lambda i: (0, 0, 0)),
                  pl.BlockSpec(w_o.shape, c2), pl.BlockSpec((1, d), c2), pl.BlockSpec((1, d), c2)],
        out_specs=pl.BlockSpec((tm, d), row),
        compiler_params=_cparams(("parallel",)),
        name="post",
    )(x, o_a, o_d, w_mg, w_br, w_o, ln_g, ln_b)


def kernel(x_prompt, x_sample, cache_k, cache_v, state_delta, state_conv, page_table, rel_bias_table,
           w_in, conv_w, lambda_q1, lambda_k1, lambda_q2, lambda_k2, diff_norm_g, a_log, dt_bias,
           delta_norm_g, w_branch, w_o, ln_g, ln_b):
    batch, seq, d = x_prompt.shape
    dec_b, dec_t, _ = x_sample.shape
    depth = w_in.shape[0]
    mix = d // 2
    assert mix == N_HEADS * HEAD_DIM and dec_t == 4 and cache_k.shape[2] == PAGE
    alpha = (2 * depth) ** 0.25
    t_attn = min(512, seq)
    c_prompt = min(64, seq)
    c_sample = 8
    nb_prompt = math.gcd(batch, 2)
    nb_sample = math.gcd(dec_b, 2)
    n_pool = cache_k.shape[1]

    bias_p, bias_sp, bias_sn, lam = _prep(rel_bias_table, lambda_q1, lambda_k1, lambda_q2, lambda_k2,
                                          t_attn, depth)
    cache_k2 = cache_k.reshape(depth, n_pool, PAGE * N_HEADS, HEAD_DIM)
    cache_v2 = cache_v.reshape(depth, n_pool, PAGE * N_HEADS, HEAD_DIM)

    hp = x_prompt.reshape(batch * seq, d)
    hs = x_sample.reshape(dec_b * dec_t, d)
    outs = {k: [] for k in ("ks", "vs", "dp", "ds", "cp", "cs")}
    k_all = jnp.zeros((depth, batch * seq * N_HEADS, HEAD_DIM), F32)
    v_all = jnp.zeros((depth, batch * seq * N_HEADS, HEAD_DIM), F32)
    zeros_buf = jnp.zeros((batch, 8, 3 * mix), F32)
    zeros_state = jnp.zeros((batch, N_HEADS, HEAD_DIM, HEAD_DIM), F32)

    for l in range(depth):
        w_main = w_in[l, :, :8 * mix].astype(BF16)
        w_bd = jnp.pad(w_in[l, :, 8 * mix:8 * mix + 2 * N_HEADS], ((0, 0), (0, LANES - 2 * N_HEADS))).astype(BF16)
        w_mg = w_in[l, :, 8 * mix + 2 * N_HEADS:].astype(BF16)
        w_kt = w_in[l, :, mix:2 * mix].T.astype(BF16)
        w_br = w_branch[l].astype(BF16)
        w_ol = w_o[l].astype(BF16)
        cw = jnp.pad(conv_w[l], ((0, 8 - CONV_W), (0, 0)))
        hpar = jnp.zeros((8, LANES), F32)
        hpar = hpar.at[0, N_HEADS:2 * N_HEADS].set(a_log[l]).at[1, N_HEADS:2 * N_HEADS].set(dt_bias[l])
        gain_a = diff_norm_g[l][None, :]
        gain_d = delta_norm_g[l][None, :]
        lng = ln_g[l][None, :]
        lnb = ln_b[l][None, :]

        qs, k_all, kt, v_all, vb, ag, pre, dg, bd = _inproj(hp, w_main, w_bd, w_kt, (k_all, v_all),
                                                            tm=t_attn, layer=l, depth=depth)
        o_a = _attn_prompt(qs, kt, vb, bias_p, ag, lam, gain_a, batch=batch, t=t_attn, layer=l)
        o_d, s_new, c_new = _delta(pre.reshape(batch, seq, 3 * mix), bd.reshape(batch, seq, LANES),
                                   dg.reshape(batch, seq, mix), zeros_buf, cw, hpar, zeros_state, gain_d,
                                   nb=nb_prompt, c=c_prompt, t_valid=seq)
        hp = _post(hp, o_a, o_d.reshape(batch * seq, mix), w_mg, w_br, w_ol, lng, lnb, alpha=alpha)
        outs["dp"].append(s_new)
        outs["cp"].append(c_new)

        q_s, k_s, v_s, ag_s, pre_s, dg_s, bd_s = _inproj(hs, w_main, w_bd, tm=dec_b * dec_t)
        o_a = _attn_sample(q_s.reshape(dec_b, dec_t, mix), k_s.reshape(dec_b, dec_t, mix),
                           v_s.reshape(dec_b, dec_t, mix), cache_k2, cache_v2, page_table,
                           bias_sp, bias_sn, ag_s.reshape(dec_b, dec_t, mix), lam, gain_a, layer=l)

        def pad_rows(a):
            a = a.reshape(dec_b, dec_t, a.shape[-1])
            return jnp.pad(a, ((0, 0), (0, c_sample - dec_t), (0, 0)))

        buf_s = jnp.pad(state_conv[l], ((0, 0), (8 - (CONV_W - 1), 0), (0, 0)))
        o_d, s_new, c_new = _delta(pad_rows(pre_s), pad_rows(bd_s), pad_rows(dg_s), buf_s, cw, hpar,
                                   state_delta[l], gain_d, nb=nb_sample, c=c_sample, t_valid=dec_t)
        o_d = o_d[:, :dec_t].reshape(dec_b * dec_t, mix)
        hs = _post(hs, o_a.reshape(dec_b * dec_t, mix), o_d, w_mg, w_br, w_ol, lng, lnb, alpha=alpha)
        outs["ks"].append(k_s.reshape(dec_b, dec_t, N_HEADS, HEAD_DIM))
        outs["vs"].append(v_s.reshape(dec_b, dec_t, N_HEADS, HEAD_DIM))
        outs["ds"].append(s_new)
        outs["cs"].append(c_new)

    st = {k: jnp.stack(v) for k, v in outs.items()}
    kv_shape = (depth, batch, seq, N_HEADS, HEAD_DIM)
    return (hp.reshape(batch, seq, d), hs.reshape(dec_b, dec_t, d), k_all.reshape(kv_shape), v_all.reshape(kv_shape),
            st["ks"], st["vs"], st["dp"], st["ds"], st["cp"], st["cs"])
```

```python
import functools
import math

import jax
import jax.numpy as jnp
import numpy as np
from jax import lax
from jax.experimental import pallas as pl
from jax.experimental.pallas import tpu as pltpu

F32 = jnp.float32
BF16 = jnp.bfloat16

N_HEADS = 4
HEAD_DIM = 128
QK_DIM = 64
CONV_W = 4
PAGE = 128
N_BUCKETS = 32
MAX_DISTANCE = 128
LN_EPS = 1e-5
NORM_EPS = 1e-6
NEG_INF = -1e30
LOG2E = math.log2(math.e)

LANES = 128
SUBLANES = 8
VMEM_LIMIT = 56 * 1024 * 1024

NN = (((1,), (0,)), ((), ()))
NT = (((1,), (1,)), ((), ()))


def _dot(a, b, dims=NN):
    return lax.dot_general(a, b, dims, preferred_element_type=F32)


def _split(a):
    hi = a.astype(BF16)
    lo = (a - hi.astype(F32)).astype(BF16)
    return hi, lo


def _dot3s(a_parts, b_parts, dims=NN):
    ah, al = a_parts
    bh, bl = b_parts
    return _dot(ah, bh, dims) + (_dot(ah, bl, dims) + _dot(al, bh, dims))


def _dot2(a_exact, b, dims=NN):
    bh, bl = _split(b)
    a = a_exact.astype(BF16)
    return _dot(a, bh, dims) + _dot(a, bl, dims)


def _silu(x):
    return x * (1.0 / (1.0 + jnp.exp(-x)))


def _sigmoid(x):
    return 1.0 / (1.0 + jnp.exp(-x))


def _softplus(x):
    return jnp.maximum(x, 0.0) + jnp.log(1.0 + jnp.exp(-jnp.abs(x)))


def _cparams(sem):
    return pltpu.CompilerParams(dimension_semantics=sem, vmem_limit_bytes=VMEM_LIMIT)


def _bucket_np(dist):
    dist = np.asarray(dist, np.int64)
    n = np.maximum(dist, 0)
    max_exact = N_BUCKETS // 2
    nf = np.maximum(n, 1).astype(np.float32)
    large = max_exact + (np.log(nf / np.float32(max_exact)) / np.float32(math.log(MAX_DISTANCE / max_exact))
                         * np.float32(N_BUCKETS - max_exact)).astype(np.int32)
    large = np.minimum(large, N_BUCKETS - 1)
    b = np.where(n < max_exact, n, large)
    return np.where(dist < 0, -1, b).astype(np.int32)


def _prep_kernel(table_ref, bkt_p_ref, bkt_sp_ref, bkt_sn_ref, lq1_ref, lk1_ref, lq2_ref, lk2_ref,
                 bias_p_ref, bias_sp_ref, bias_sn_ref, lam_ref, *, depth):
    def lookup(bkt, h, unit):
        far = table_ref[N_BUCKETS - 1, h]
        out = jnp.full(bkt.shape, NEG_INF, F32)
        for b in range(N_BUCKETS):
            out = jnp.where(bkt == b, (table_ref[b, h] - far) * unit, out)
        return out

    for h in range(N_HEADS):
        for kind in range(2):
            bias_p_ref[h, kind] = lookup(bkt_p_ref[kind], h, LOG2E)
        bias_sp_ref[h * 8:(h + 1) * 8, :] = lookup(bkt_sp_ref[...], h, 1.0)
        bias_sn_ref[h * 8:(h + 1) * 8, :] = lookup(bkt_sn_ref[...], h, 1.0)

    rows = lax.broadcasted_iota(jnp.int32, (SUBLANES, LANES), 0)
    lam = jnp.zeros((SUBLANES, LANES), F32)
    for l in range(depth):
        lam_init = 0.8 - 0.6 * math.exp(-0.3 * l)
        s1 = jnp.sum(lq1_ref[l:l + 1, :] * lk1_ref[l:l + 1, :], axis=1, keepdims=True)
        s2 = jnp.sum(lq2_ref[l:l + 1, :] * lk2_ref[l:l + 1, :], axis=1, keepdims=True)
        val = jnp.exp(s1) - jnp.exp(s2) + lam_init
        lam = jnp.where(rows == l, val, lam)
    lam_ref[...] = lam


def _prep(table, lq1, lk1, lq2, lk2, t, depth):
    qry = np.arange(t)[:, None]
    key = np.arange(t)[None, :]
    bkt_p = np.stack([_bucket_np(qry - key), _bucket_np(t + qry - key)])
    tq = np.tile(np.arange(4), 2)[:, None]
    bkt_sp = _bucket_np(PAGE + tq - np.arange(PAGE)[None, :])
    cn = np.arange(LANES)[None, :]
    bkt_sn = np.where(cn < 4, _bucket_np(tq - cn), -1).astype(np.int32)
    vm = pl.BlockSpec(memory_space=pltpu.VMEM)
    return pl.pallas_call(
        functools.partial(_prep_kernel, depth=depth),
        out_shape=(jax.ShapeDtypeStruct((N_HEADS, 2, t, t), F32),
                   jax.ShapeDtypeStruct((N_HEADS * 8, PAGE), F32),
                   jax.ShapeDtypeStruct((N_HEADS * 8, LANES), F32),
                   jax.ShapeDtypeStruct((SUBLANES, LANES), F32)),
        in_specs=[pl.BlockSpec(memory_space=pltpu.SMEM), vm, vm, vm, vm, vm, vm, vm],
        out_specs=(vm, vm, vm, vm),
        compiler_params=pltpu.CompilerParams(vmem_limit_bytes=VMEM_LIMIT),
        name="prep",
    )(table, jnp.asarray(bkt_p), jnp.asarray(bkt_sp), jnp.asarray(bkt_sn), lq1, lk1, lq2, lk2)


def _inproj_kernel(x_ref, w_ref, wbd_ref, *refs, mix, tm, prompt, n_prev):
    if prompt:
        wt_ref = refs[0]
        qs_ref, k_ref, kt_ref, v_ref, vb_ref, ag_ref, pre_ref, dg_ref, bd_ref = refs[1 + n_prev:]
    else:
        q_ref, k_ref, v_ref, ag_ref, pre_ref, dg_ref, bd_ref = refs
    xb = x_ref[...].astype(BF16)

    def seg(i, n=1):
        return _dot(xb, w_ref[:, i * mix:(i + n) * mix])

    scale = QK_DIM ** -0.5
    k = seg(1)
    v = seg(2)
    if prompt:
        q = seg(0) * (scale * LOG2E)
        lane = lax.broadcasted_iota(jnp.int32, q.shape, 1)
        first = (lane % HEAD_DIM) < QK_DIM
        qs_ref[0] = jnp.where(first, q, 0.0).astype(BF16)
        qs_ref[1] = jnp.where(first, 0.0, q).astype(BF16)
        kt_ref[...] = _dot(wt_ref[...], xb, NT).astype(BF16)
        vb_ref[...] = v.astype(BF16)
        for h in range(N_HEADS):
            k_ref[pl.ds(h, tm, stride=N_HEADS), :] = k[:, h * HEAD_DIM:(h + 1) * HEAD_DIM]
            v_ref[pl.ds(h, tm, stride=N_HEADS), :] = v[:, h * HEAD_DIM:(h + 1) * HEAD_DIM]
    else:
        q_ref[...] = seg(0) * scale
        k_ref[...] = k
        v_ref[...] = v
    ag_ref[...] = _silu(seg(3)).astype(ag_ref.dtype)
    pre_ref[...] = seg(4, 3)
    dg_ref[...] = _silu(seg(7)).astype(dg_ref.dtype)
    bd_ref[...] = _dot(xb, wbd_ref[...])


def _inproj(x, w_main, w_bd, w_t=None, kv_prev=None, *, tm, layer=0, depth=1):
    m, d = x.shape
    mix = d // 2
    prompt = w_t is not None
    row = lambda i: (i, 0)
    col = lambda i: (0, i)
    full = lambda i: (0, 0)
    o_mix = pl.BlockSpec((tm, mix), row)
    f_mix = jax.ShapeDtypeStruct((m, mix), F32)
    b_mix = jax.ShapeDtypeStruct((m, mix), BF16)
    in_specs = [pl.BlockSpec((tm, d), row), pl.BlockSpec(w_main.shape, full), pl.BlockSpec(w_bd.shape, full)]
    args = [x, w_main, w_bd]
    aliases = {}
    n_prev = 0
    if prompt:
        in_specs.append(pl.BlockSpec(w_t.shape, full))
        args.append(w_t)
        if kv_prev is not None:
            n_prev = 2
            in_specs += [pl.BlockSpec(memory_space=pl.ANY)] * 2
            args += list(kv_prev)
            aliases = {4: 1, 5: 3}
        kv_all = jax.ShapeDtypeStruct((depth, m * N_HEADS, HEAD_DIM), F32)
        kv_spec = pl.BlockSpec((None, tm * N_HEADS, HEAD_DIM), lambda i: (layer, i, 0))
        shapes = [jax.ShapeDtypeStruct((2, m, mix), BF16), kv_all, jax.ShapeDtypeStruct((mix, m), BF16),
                  kv_all, b_mix]
        specs = [pl.BlockSpec((2, tm, mix), lambda i: (0, i, 0)), kv_spec, pl.BlockSpec((mix, tm), col),
                 kv_spec, o_mix]
    else:
        shapes = [f_mix] * 3
        specs = [o_mix] * 3
    gate = b_mix if prompt else f_mix
    shapes += [gate, jax.ShapeDtypeStruct((m, 3 * mix), F32), gate, jax.ShapeDtypeStruct((m, LANES), F32)]
    specs += [o_mix, pl.BlockSpec((tm, 3 * mix), row), o_mix, pl.BlockSpec((tm, LANES), row)]
    return pl.pallas_call(
        functools.partial(_inproj_kernel, mix=mix, tm=tm, prompt=prompt, n_prev=n_prev),
        out_shape=tuple(shapes),
        grid=(m // tm,),
        in_specs=in_specs,
        out_specs=tuple(specs),
        input_output_aliases=aliases,
        compiler_params=_cparams(("parallel",)),
        name="inproj",
    )(*args)


def _head_norm_gate(o, gain, gate, scale):
    ms = jnp.mean(o * o, axis=-1, keepdims=True)
    return o * lax.rsqrt(ms + NORM_EPS) * (gain * scale) * gate


def _attn_prompt_kernel(ii_ref, jj_ref, qs_ref, kt_ref, vb_ref, bias_ref, ag_ref, lam_ref, gain_ref,
                        o_ref, m_ref, acc_ref, *, t, rw, layer, out_scale):
    s_idx = pl.program_id(1)
    i = ii_ref[s_idx]
    j = jj_ref[s_idx]
    units = [(h, c) for h in range(N_HEADS) for c in range(2 * t // rw)]

    @pl.when(j == 0)
    def _():
        m_ref[...] = jnp.full(m_ref.shape, NEG_INF, F32)
        acc_ref[...] = jnp.zeros(acc_ref.shape, F32)

    def scores(u, kind):
        h, c = u
        r0 = c * rw
        cols = slice(h * HEAD_DIM, (h + 1) * HEAD_DIM)
        q = qs_ref[r0 // t, r0 % t:r0 % t + rw, cols]
        s = _dot(q, kt_ref[cols, :])
        if kind is not None:
            s = s + bias_ref[h, kind, r0 % t:r0 % t + rw, :]
        return s

    def update(kind):
        ones = jnp.ones((t, LANES), BF16)
        v_aug = [jnp.concatenate([vb_ref[:, h * HEAD_DIM:(h + 1) * HEAD_DIM], ones], axis=1)
                 for h in range(N_HEADS)]
        s_next = scores(units[0], kind)
        for n, u in enumerate(units):
            h, c = u
            s = s_next
            if n + 1 < len(units):
                s_next = scores(units[n + 1], kind)
            rows = slice(c * rw, (c + 1) * rw)
            m_prev = m_ref[h, rows, :]
            m_new = jnp.maximum(m_prev, jnp.max(s, axis=1, keepdims=True))
            alpha = jnp.exp2(m_prev - m_new)
            p = jnp.exp2(s - jnp.concatenate([m_new] * (t // LANES), axis=1)).astype(BF16)
            acc_ref[h, rows, :] = (jnp.concatenate([alpha, alpha], axis=1) * acc_ref[h, rows, :]
                                   + _dot(p, v_aug[h]))
            m_ref[h, rows, :] = m_new

    @pl.when(j < i - 1)
    def _():
        update(None)

    @pl.when(j == i - 1)
    def _():
        update(1)

    @pl.when(j == i)
    def _():
        update(0)
        lam = lam_ref[layer:layer + 1, :]
        for h in range(N_HEADS):
            cols = slice(h * HEAD_DIM, (h + 1) * HEAD_DIM)
            o = acc_ref[h, :, :HEAD_DIM] / acc_ref[h, :, HEAD_DIM:]
            od = o[:t] - lam * o[t:]
            o_ref[:, cols] = _head_norm_gate(od, gain_ref[...], ag_ref[:, cols].astype(F32),
                                             out_scale).astype(o_ref.dtype)


def _attn_prompt(qs, kt, vb, bias_p, ag, lam, gain, *, batch, t, layer):
    m, mix = vb.shape
    n_t = m // batch // t
    ii = np.concatenate([np.full(i + 1, i) for i in range(n_t)]).astype(np.int32)
    jj = np.concatenate([np.arange(i + 1) for i in range(n_t)]).astype(np.int32)
    lam_init = 0.8 - 0.6 * math.exp(-0.3 * layer)
    rw = min(t, 4 * LANES)
    q_rows = lambda b, s, ii, jj: (b * n_t + ii[s], 0)
    k_rows = lambda b, s, ii, jj: (b * n_t + jj[s], 0)
    k_cols = lambda b, s, ii, jj: (0, b * n_t + jj[s])
    const2 = lambda b, s, ii, jj: (0, 0)
    grid_spec = pltpu.PrefetchScalarGridSpec(
        num_scalar_prefetch=2,
        grid=(batch, len(ii)),
        in_specs=[pl.BlockSpec((2, t, mix), lambda b, s, ii, jj: (0, b * n_t + ii[s], 0)),
                  pl.BlockSpec((mix, t), k_cols),
                  pl.BlockSpec((t, mix), k_rows),
                  pl.BlockSpec(memory_space=pltpu.VMEM),
                  pl.BlockSpec((t, mix), q_rows),
                  pl.BlockSpec((SUBLANES, LANES), const2),
                  pl.BlockSpec((1, HEAD_DIM), const2)],
        out_specs=pl.BlockSpec((t, mix), q_rows),
        scratch_shapes=[pltpu.VMEM((N_HEADS, 2 * t, LANES), F32),
                        pltpu.VMEM((N_HEADS, 2 * t, HEAD_DIM + LANES), F32)],
    )
    return pl.pallas_call(
        functools.partial(_attn_prompt_kernel, t=t, rw=rw, layer=layer, out_scale=1.0 - lam_init),
        out_shape=jax.ShapeDtypeStruct((m, mix), BF16),
        grid_spec=grid_spec,
        compiler_params=_cparams(("parallel", "arbitrary")),
        name="attn_prompt",
    )(jnp.asarray(ii), jnp.asarray(jj), qs, kt, vb, bias_p, ag, lam, gain)


def _attn_sample_kernel(*refs, g, layer, out_scale):
    pt_ref = refs[0]
    q_ref, kn_ref, vn_ref = refs[1:4]
    k_refs = refs[4:4 + g]
    v_refs = refs[4 + g:4 + 2 * g]
    bias_sp_ref, bias_sn_ref, ag_ref, lam_ref, gain_ref = refs[4 + 2 * g:9 + 2 * g]
    o_ref = refs[9 + 2 * g]
    wq_ref, m_ref, l_ref, acc_ref, own_ref = refs[10 + 2 * g:]
    del pt_ref
    step = pl.program_id(1)
    n_steps = pl.num_programs(1)
    rows = N_HEADS * 8
    width = N_HEADS * HEAD_DIM

    @pl.when(step == 0)
    def _():
        r = lax.broadcasted_iota(jnp.int32, (rows, width), 0)
        c = lax.broadcasted_iota(jnp.int32, (rows, width), 1)
        keep = (c // HEAD_DIM == r // 8) & ((c % HEAD_DIM) // QK_DIM == (r // 4) % 2)
        wq = jnp.zeros((rows, width), F32)
        for tq in range(4):
            wq = jnp.where(keep & (r % 4 == tq), q_ref[tq:tq + 1, :], wq)
        wq_ref[...] = wq.astype(BF16)
        m_ref[...] = jnp.full(m_ref.shape, NEG_INF, F32)
        l_ref[...] = jnp.zeros(l_ref.shape, F32)
        acc_ref[...] = jnp.zeros(acc_ref.shape, F32)
        own_ref[...] = jnp.zeros(own_ref.shape, F32)

    def update(s, v_list):
        m_prev = m_ref[...]
        m_new = jnp.maximum(m_prev, jnp.max(s, axis=1, keepdims=True))
        alpha = jnp.exp(m_prev - m_new)
        p = jnp.exp(s - m_new)
        l_ref[...] = alpha * l_ref[...] + jnp.sum(p, axis=1, keepdims=True)
        p = p.astype(BF16)
        pv = None
        for idx, vb in enumerate(v_list):
            w = vb.shape[0]
            part = _dot(p[:, idx * PAGE:idx * PAGE + w] if w == PAGE else p[:, :w], vb)
            pv = part if pv is None else pv + part
        acc_ref[...] = alpha * acc_ref[...] + pv
        m_ref[...] = m_new

    wq = wq_ref[...]

    def page(ref):
        return jnp.concatenate([ref[pl.ds(h, PAGE, stride=N_HEADS), :] for h in range(N_HEADS)],
                               axis=1).astype(BF16)

    def page_scores():
        return [_dot(wq, page(k_refs[idx]), NT) for idx in range(g)]

    @pl.when(step < n_steps - 1)
    def _():
        update(jnp.concatenate(page_scores(), axis=1), [page(v_refs[idx]) for idx in range(g)])

    @pl.when(step == n_steps - 1)
    def _():
        sc = page_scores()
        sc[g - 1] = sc[g - 1] + bias_sp_ref[...]
        update(jnp.concatenate(sc, axis=1), [page(v_refs[idx]) for idx in range(g)])
        own_ref[0, 0:4, :] = kn_ref[...]
        own_ref[1, 0:4, :] = vn_ref[...]
        update(_dot(wq, own_ref[0].astype(BF16), NT) + bias_sn_ref[...], [own_ref[1].astype(BF16)])
        o = acc_ref[...] / l_ref[...]
        lam = lam_ref[layer:layer + 1, :]
        for h in range(N_HEADS):
            blk = o[h * 8:(h + 1) * 8, h * HEAD_DIM:(h + 1) * HEAD_DIM]
            od = (blk - lam * pltpu.roll(blk, 4, axis=0))[0:4]
            gate = ag_ref[:, h * HEAD_DIM:(h + 1) * HEAD_DIM].astype(F32)
            o_ref[:, h * HEAD_DIM:(h + 1) * HEAD_DIM] = _head_norm_gate(
                od, gain_ref[...], gate, out_scale).astype(o_ref.dtype)


def _attn_sample(q, k_new, v_new, cache_k, cache_v, page_table, bias_sp, bias_sn, ag, lam, gain, *, layer):
    bsz, tq, width = q.shape
    n_pages = page_table.shape[1]
    g = math.gcd(n_pages, 32)
    lam_init = 0.8 - 0.6 * math.exp(-0.3 * layer)
    per_b = lambda b, s, pt: (b, 0, 0)
    const2 = lambda b, s, pt: (0, 0)

    def page_spec(idx):
        return pl.BlockSpec((None, None, PAGE * N_HEADS, HEAD_DIM),
                            lambda b, s, pt: (layer, pt[b, s * g + idx], 0, 0))

    grid_spec = pltpu.PrefetchScalarGridSpec(
        num_scalar_prefetch=1,
        grid=(bsz, n_pages // g),
        in_specs=([pl.BlockSpec((None, tq, width), per_b)] * 3
                  + [page_spec(idx) for idx in range(g)] * 2
                  + [pl.BlockSpec(bias_sp.shape, const2), pl.BlockSpec(bias_sn.shape, const2),
                     pl.BlockSpec((None, tq, width), per_b),
                     pl.BlockSpec((SUBLANES, LANES), const2), pl.BlockSpec((1, HEAD_DIM), const2)]),
        out_specs=pl.BlockSpec((None, tq, width), per_b),
        scratch_shapes=[pltpu.VMEM((N_HEADS * 8, width), BF16), pltpu.VMEM((N_HEADS * 8, 1), F32),
                        pltpu.VMEM((N_HEADS * 8, 1), F32), pltpu.VMEM((N_HEADS * 8, width), F32),
                        pltpu.VMEM((2, PAGE, width), F32)],
    )
    return pl.pallas_call(
        functools.partial(_attn_sample_kernel, g=g, layer=layer, out_scale=1.0 - lam_init),
        out_shape=jax.ShapeDtypeStruct((bsz, tq, width), F32),
        grid_spec=grid_spec,
        compiler_params=_cparams(("parallel", "arbitrary")),
        name="attn_sample",
    )(page_table, q, k_new, v_new, *([cache_k] * g), *([cache_v] * g), bias_sp, bias_sn, ag, lam, gain)


def _unit_lower_inverses(lmats, c):
    r = lax.broadcasted_iota(jnp.int32, (c, c), 0)
    q = lax.broadcasted_iota(jnp.int32, (c, c), 1)
    eye = (r == q).astype(F32)
    base = min(c, 16)
    same_base = r // base == q // base
    zero = jnp.zeros((c, c), BF16)

    def masked(parts, mask, sign):
        return tuple(jnp.where(mask, sign * part, zero) for part in parts)

    l_parts = [_split(lm) for lm in lmats]
    p_parts = [masked(lp, same_base, -1) for lp in l_parts]
    xs = [eye + jnp.where(same_base, -lm, 0.0) for lm in lmats]
    for _ in range(int(math.log2(base)) - 1):
        p_parts = [_split(_dot3s(pp, pp)) for pp in p_parts]
        xs = [x + _dot3s(_split(x), pp) for x, pp in zip(xs, p_parts)]
    x_parts = [_split(x) for x in xs]
    s = base
    while s < c:
        below = (r // (2 * s) == q // (2 * s)) & (r // s != q // s)
        t_parts = [_split(_dot3s(masked(lp, below, 1), xp)) for lp, xp in zip(l_parts, x_parts)]
        xs = [x - _dot3s(xp, tp) for x, xp, tp in zip(xs, x_parts, t_parts)]
        x_parts = [_split(x) for x in xs]
        s *= 2
    return x_parts


def _delta_kernel(pre_ref, bd_ref, dg_ref, buf_ref, cw_ref, hp_ref, s0_ref, gain_ref,
                  o_ref, s_out_ref, conv_out_ref, s_ref, xcat_ref, *, nb, c, mix, t_valid):
    n = pl.program_id(1)
    n_chunks = pl.num_programs(1)
    units = [(b, h) for b in range(nb) for h in range(N_HEADS)]

    @pl.when(n == 0)
    def _():
        s_ref[...] = s0_ref[...]
        xcat_ref[:, 0:8, :] = buf_ref[...]

    r = lax.broadcasted_iota(jnp.int32, (c, c), 0)
    q_i = lax.broadcasted_iota(jnp.int32, (c, c), 1)
    incl = r >= q_i
    strict = r > q_i
    incl_f = incl.astype(F32)
    sel = (lax.broadcasted_iota(jnp.int32, (8, LANES), 1)
           == lax.broadcasted_iota(jnp.int32, (8, LANES), 0) + 4).astype(F32)
    eye_h = (lax.broadcasted_iota(jnp.int32, (HEAD_DIM, HEAD_DIM), 0)
             == lax.broadcasted_iota(jnp.int32, (HEAD_DIM, HEAD_DIM), 1)).astype(BF16)
    last_valid = t_valid - (t_valid - 1) // c * c

    ys, beta_ts, gcum_ts, gcum_rows = [], [], [], []
    for b in range(nb):
        xcat_ref[b, 8:8 + c, :] = pre_ref[b]
        y = xcat_ref[b, 5:5 + c, :] * cw_ref[0:1, :]
        for j in range(1, CONV_W):
            y = y + xcat_ref[b, 5 + j:5 + j + c, :] * cw_ref[j:j + 1, :]
        ys.append(_silu(y))
        conv_out_ref[b] = xcat_ref[b, last_valid:last_valid + 8, :]
        xcat_ref[b, 0:8, :] = xcat_ref[b, c:c + 8, :]

        bd = bd_ref[b]
        beta_t = _sigmoid(bd)
        g_t = -jnp.exp(hp_ref[0:1, :]) * _softplus(bd + hp_ref[1:2, :])
        if t_valid % c:
            row_ok = (lax.broadcasted_iota(jnp.int32, bd.shape, 0) + n * c) < t_valid
            beta_t = jnp.where(row_ok, beta_t, 0.0)
            g_t = jnp.where(row_ok, g_t, 0.0)
        beta_ts.append(beta_t)
        gcum_ts.append(_dot2(incl_f, g_t))
    for b in range(nb):
        gcum_rows.append(_dot2(sel, gcum_ts[b], NT))

    qn, kn, vb_, kb, e_g, decay, g_last, gcum = {}, {}, {}, {}, {}, {}, {}, {}
    for u in units:
        b, h = u
        lo, hi = h * HEAD_DIM, (h + 1) * HEAD_DIM
        qh = ys[b][:, lo:hi]
        kh = ys[b][:, mix + lo:mix + hi]
        qn[u] = qh * lax.rsqrt(jnp.sum(qh * qh, axis=-1, keepdims=True) + NORM_EPS) * (HEAD_DIM ** -0.5)
        kn[u] = kh * lax.rsqrt(jnp.sum(kh * kh, axis=-1, keepdims=True) + NORM_EPS)
        beta = beta_ts[b][:, h:h + 1]
        gcum[u] = gcum_ts[b][:, 4 + h:5 + h]
        g_last[u] = gcum_ts[b][c - 1:c, 4 + h:5 + h]
        diff = gcum[u] - gcum_rows[b][h:h + 1, :]
        decay[u] = jnp.where(incl, jnp.exp(jnp.where(incl, diff, 0.0)), 0.0)
        kb[u] = kn[u] * beta
        vb_[u] = ys[b][:, 2 * mix + lo:2 * mix + hi] * beta
        e_g[u] = jnp.exp(gcum[u])
    knb = {u: kn[u].astype(BF16) for u in units}
    lmat = {u: jnp.where(strict, _dot(kb[u].astype(BF16), knb[u], NT) * decay[u], 0.0) for u in units}
    a_qk = {u: jnp.where(incl, _dot(qn[u].astype(BF16), knb[u], NT) * decay[u], 0.0).astype(BF16)
            for u in units}

    tinv = dict(zip(units, _unit_lower_inverses([lmat[u] for u in units], c)))
    uw = {u: _dot3s(tinv[u], _split(jnp.concatenate([vb_[u], kb[u] * e_g[u]], axis=1))) for u in units}

    s_old = {u: s_ref[u[0], u[1]] for u in units}
    s_b = {u: s_old[u].astype(BF16) for u in units}
    ws = {u: _dot(uw[u][:, HEAD_DIM:].astype(BF16), s_b[u]) for u in units}
    qs = {u: _dot((qn[u] * e_g[u]).astype(BF16), s_b[u]) for u in units}
    k_tail_t = {u: _dot(eye_h, (kn[u] * jnp.exp(g_last[u] - gcum[u])).astype(BF16), NT).astype(BF16)
                for u in units}
    vnb = {u: (uw[u][:, :HEAD_DIM] - ws[u]).astype(BF16) for u in units}
    o = {u: qs[u] + _dot(a_qk[u], vnb[u]) for u in units}
    for u in units:
        s_ref[u[0], u[1]] = s_old[u] * jnp.exp(g_last[u]) + _dot(k_tail_t[u], vnb[u])
    for u in units:
        b, h = u
        lo, hi = h * HEAD_DIM, (h + 1) * HEAD_DIM
        ms = jnp.mean(o[u] * o[u], axis=-1, keepdims=True)
        o_n = o[u] * lax.rsqrt(ms + NORM_EPS) * gain_ref[...]
        o_ref[b, :, lo:hi] = (o_n * dg_ref[b, :, lo:hi].astype(F32)).astype(o_ref.dtype)

    @pl.when(n == n_chunks - 1)
    def _():
        s_out_ref[...] = s_ref[...]


def _delta(pre, bd, dg, conv_buf, conv_w, hp, s0, gain, *, nb, c, t_valid):
    batch, t_pad, width3 = pre.shape
    mix = width3 // 3
    n_chunks = t_pad // c
    tok = lambda b, n: (b, n, 0)
    per_b3 = lambda b, n: (b, 0, 0)
    per_b4 = lambda b, n: (b, 0, 0, 0)
    const2 = lambda b, n: (0, 0)
    o_d, s_new, conv_out = pl.pallas_call(
        functools.partial(_delta_kernel, nb=nb, c=c, mix=mix, t_valid=t_valid),
        out_shape=(jax.ShapeDtypeStruct((batch, t_pad, mix), dg.dtype),
                   jax.ShapeDtypeStruct(s0.shape, F32),
                   jax.ShapeDtypeStruct((batch, 8, width3), F32)),
        grid=(batch // nb, n_chunks),
        in_specs=[pl.BlockSpec((nb, c, width3), tok), pl.BlockSpec((nb, c, LANES), tok),
                  pl.BlockSpec((nb, c, mix), tok),
                  pl.BlockSpec((nb, 8, width3), per_b3),
                  pl.BlockSpec((8, width3), const2),
                  pl.BlockSpec((8, LANES), const2),
                  pl.BlockSpec((nb, N_HEADS, HEAD_DIM, HEAD_DIM), per_b4),
                  pl.BlockSpec((1, HEAD_DIM), const2)],
        out_specs=(pl.BlockSpec((nb, c, mix), tok),
                   pl.BlockSpec((nb, N_HEADS, HEAD_DIM, HEAD_DIM), per_b4),
                   pl.BlockSpec((nb, 8, width3), per_b3)),
        scratch_shapes=[pltpu.VMEM((nb, N_HEADS, HEAD_DIM, HEAD_DIM), F32),
                        pltpu.VMEM((nb, c + 8, width3), F32)],
        compiler_params=_cparams(("parallel", "arbitrary")),
        name="delta",
    )(pre, bd, dg, conv_buf, conv_w, hp, s0, gain)
    return o_d, s_new, conv_out[:, 5:8, :]


def _post_kernel(x_ref, oa_ref, od_ref, wmg_ref, wbr_ref, wo_ref, lng_ref, lnb_ref, y_ref, *, d, alpha):
    x = x_ref[...]
    gates = _sigmoid(_dot(x.astype(BF16), wmg_ref[...]))
    h_a = _dot(oa_ref[...].astype(BF16), wbr_ref[0])
    h_d = _dot(od_ref[...].astype(BF16), wbr_ref[1])
    merged = gates[:, :d] * h_a + gates[:, d:] * h_d
    y = _dot(merged.astype(BF16), wo_ref[...])
    z = alpha * x + y
    mu = jnp.mean(z, axis=-1, keepdims=True)
    zc = z - mu
    var = jnp.mean(zc * zc, axis=-1, keepdims=True)
    y_ref[...] = zc * lax.rsqrt(var + LN_EPS) * lng_ref[...] + lnb_ref[...]


def _post(x, o_a, o_d, w_mg, w_br, w_o, ln_g, ln_b, *, alpha):
    m, d = x.shape
    tm = min(512, m)
    row = lambda i: (i, 0)
    c2 = lambda i: (0, 0)
    return pl.pallas_call(
        functools.partial(_post_kernel, d=d, alpha=alpha),
        out_shape=jax.ShapeDtypeStruct((m, d), F32),
        grid=(m // tm,),
        in_specs=[pl.BlockSpec((tm, d), row), pl.BlockSpec((tm, d // 2), row), pl.BlockSpec((tm, d // 2), row),
                  pl.BlockSpec(w_mg.shape, c2), pl.BlockSpec(w_br.shape, lambda i: (0, 0, 0)),
                  pl.BlockSpec(w_o.shape, c2), pl.BlockSpec((1, d), c2), pl.BlockSpec((1, d), c2)],
        out_specs=pl.BlockSpec((tm, d), row),
        compiler_params=_cparams(("parallel",)),
        name="post",
    )(x, o_a, o_d, w_mg, w_br, w_o, ln_g, ln_b)


def kernel(x_prompt, x_sample, cache_k, cache_v, state_delta, state_conv, page_table, rel_bias_table,
           w_in, conv_w, lambda_q1, lambda_k1, lambda_q2, lambda_k2, diff_norm_g, a_log, dt_bias,
           delta_norm_g, w_branch, w_o, ln_g, ln_b):
    batch, seq, d = x_prompt.shape
    dec_b, dec_t, _ = x_sample.shape
    depth = w_in.shape[0]
    mix = d // 2
    assert mix == N_HEADS * HEAD_DIM and dec_t == 4 and cache_k.shape[2] == PAGE
    alpha = (2 * depth) ** 0.25
    t_attn = min(512, seq)
    c_prompt = min(64, seq)
    c_sample = 8
    nb_prompt = math.gcd(batch, 2)
    nb_sample = math.gcd(dec_b, 2)
    n_pool = cache_k.shape[1]

    bias_p, bias_sp, bias_sn, lam = _prep(rel_bias_table, lambda_q1, lambda_k1, lambda_q2, lambda_k2,
                                          t_attn, depth)
    cache_k2 = cache_k.reshape(depth, n_pool, PAGE * N_HEADS, HEAD_DIM)
    cache_v2 = cache_v.reshape(depth, n_pool, PAGE * N_HEADS, HEAD_DIM)

    hp = x_prompt.reshape(batch * seq, d)
    hs = x_sample.reshape(dec_b * dec_t, d)
    outs = {k: [] for k in ("ks", "vs", "dp", "ds", "cp", "cs")}
    k_all = jnp.zeros((depth, batch * seq * N_HEADS, HEAD_DIM), F32)
    v_all = jnp.zeros((depth, batch * seq * N_HEADS, HEAD_DIM), F32)
    zeros_buf = jnp.zeros((batch, 8, 3 * mix), F32)
    zeros_state = jnp.zeros((batch, N_HEADS, HEAD_DIM, HEAD_DIM), F32)

    for l in range(depth):
        w_main = w_in[l, :, :8 * mix].astype(BF16)
        w_bd = jnp.pad(w_in[l, :, 8 * mix:8 * mix + 2 * N_HEADS], ((0, 0), (0, LANES - 2 * N_HEADS))).astype(BF16)
        w_mg = w_in[l, :, 8 * mix + 2 * N_HEADS:].astype(BF16)
        w_kt = w_in[l, :, mix:2 * mix].T.astype(BF16)
        w_br = w_branch[l].astype(BF16)
        w_ol = w_o[l].astype(BF16)
        cw = jnp.pad(conv_w[l], ((0, 8 - CONV_W), (0, 0)))
        hpar = jnp.zeros((8, LANES), F32)
        hpar = hpar.at[0, N_HEADS:2 * N_HEADS].set(a_log[l]).at[1, N_HEADS:2 * N_HEADS].set(dt_bias[l])
        gain_a = diff_norm_g[l][None, :]
        gain_d = delta_norm_g[l][None, :]
        lng = ln_g[l][None, :]
        lnb = ln_b[l][None, :]

        qs, k_all, kt, v_all, vb, ag, pre, dg, bd = _inproj(hp, w_main, w_bd, w_kt, (k_all, v_all),
                                                            tm=t_attn, layer=l, depth=depth)
        o_a = _attn_prompt(qs, kt, vb, bias_p, ag, lam, gain_a, batch=batch, t=t_attn, layer=l)
        o_d, s_new, c_new = _delta(pre.reshape(batch, seq, 3 * mix), bd.reshape(batch, seq, LANES),
                                   dg.reshape(batch, seq, mix), zeros_buf, cw, hpar, zeros_state, gain_d,
                                   nb=nb_prompt, c=c_prompt, t_valid=seq)
        hp = _post(hp, o_a, o_d.reshape(batch * seq, mix), w_mg, w_br, w_ol, lng, lnb, alpha=alpha)
        outs["dp"].append(s_new)
        outs["cp"].append(c_new)

        q_s, k_s, v_s, ag_s, pre_s, dg_s, bd_s = _inproj(hs, w_main, w_bd, tm=dec_b * dec_t)
        o_a = _attn_sample(q_s.reshape(dec_b, dec_t, mix), k_s.reshape(dec_b, dec_t, mix),
                           v_s.reshape(dec_b, dec_t, mix), cache_k2, cache_v2, page_table,
                           bias_sp, bias_sn, ag_s.reshape(dec_b, dec_t, mix), lam, gain_a, layer=l)

        def pad_rows(a):
            a = a.reshape(dec_b, dec_t, a.shape[-1])
            return jnp.pad(a, ((0, 0), (0, c_sample - dec_t), (0, 0)))

        buf_s = jnp.pad(state_conv[l], ((0, 0), (8 - (CONV_W - 1), 0), (0, 0)))
        o_d, s_new, c_new = _delta(pad_rows(pre_s), pad_rows(bd_s), pad_rows(dg_s), buf_s, cw, hpar,
                                   state_delta[l], gain_d, nb=nb_sample, c=c_sample, t_valid=dec_t)
        o_d = o_d[:, :dec_t].reshape(dec_b * dec_t, mix)
        hs = _post(hs, o_a.reshape(dec_b * dec_t, mix), o_d, w_mg, w_br, w_ol, lng, lnb, alpha=alpha)
        outs["ks"].append(k_s.reshape(dec_b, dec_t, N_HEADS, HEAD_DIM))
        outs["vs"].append(v_s.reshape(dec_b, dec_t, N_HEADS, HEAD_DIM))
        outs["ds"].append(s_new)
        outs["cs"].append(c_new)

    st = {k: jnp.stack(v) for k, v in outs.items()}
    kv_shape = (depth, batch, seq, N_HEADS, HEAD_DIM)
    return (hp.reshape(batch, seq, d), hs.reshape(dec_b, dec_t, d), k_all.reshape(kv_shape), v_all.reshape(kv_shape),
            st["ks"], st["vs"], st["dp"], st["ds"], st["cp"], st["cs"])
```

```python
import functools
import math

import jax
import jax.numpy as jnp
import numpy as np
from jax import lax
from jax.experimental import pallas as pl
from jax.experimental.pallas import tpu as pltpu

F32 = jnp.float32
BF16 = jnp.bfloat16

N_HEADS = 4
HEAD_DIM = 128
QK_DIM = 64
CONV_W = 4
PAGE = 128
N_BUCKETS = 32
MAX_DISTANCE = 128
LN_EPS = 1e-5
NORM_EPS = 1e-6
NEG_INF = -1e30
LOG2E = math.log2(math.e)

LANES = 128
SUBLANES = 8
VMEM_LIMIT = 56 * 1024 * 1024

NN = (((1,), (0,)), ((), ()))
NT = (((1,), (1,)), ((), ()))


def _dot(a, b, dims=NN):
    return lax.dot_general(a, b, dims, preferred_element_type=F32)


def _split(a):
    hi = a.astype(BF16)
    lo = (a - hi.astype(F32)).astype(BF16)
    return hi, lo


def _dot3s(a_parts, b_parts, dims=NN):
    ah, al = a_parts
    bh, bl = b_parts
    return _dot(ah, bh, dims) + (_dot(ah, bl, dims) + _dot(al, bh, dims))


def _dot2(a_exact, b, dims=NN):
    bh, bl = _split(b)
    a = a_exact.astype(BF16)
    return _dot(a, bh, dims) + _dot(a, bl, dims)


def _silu(x):
    return x * (1.0 / (1.0 + jnp.exp(-x)))


def _sigmoid(x):
    return 1.0 / (1.0 + jnp.exp(-x))


def _softplus(x):
    return jnp.maximum(x, 0.0) + jnp.log(1.0 + jnp.exp(-jnp.abs(x)))


def _cparams(sem):
    return pltpu.CompilerParams(dimension_semantics=sem, vmem_limit_bytes=VMEM_LIMIT)


def _bucket_np(dist):
    dist = np.asarray(dist, np.int64)
    n = np.maximum(dist, 0)
    max_exact = N_BUCKETS // 2
    nf = np.maximum(n, 1).astype(np.float32)
    large = max_exact + (np.log(nf / np.float32(max_exact)) / np.float32(math.log(MAX_DISTANCE / max_exact))
                         * np.float32(N_BUCKETS - max_exact)).astype(np.int32)
    large = np.minimum(large, N_BUCKETS - 1)
    b = np.where(n < max_exact, n, large)
    return np.where(dist < 0, -1, b).astype(np.int32)


def _prep_kernel(table_ref, bkt_p_ref, bkt_sp_ref, bkt_sn_ref, lq1_ref, lk1_ref, lq2_ref, lk2_ref,
                 bias_p_ref, bias_sp_ref, bias_sn_ref, lam_ref, *, depth):
    def lookup(bkt, h, unit):
        far = table_ref[N_BUCKETS - 1, h]
        out = jnp.full(bkt.shape, NEG_INF, F32)
        for b in range(N_BUCKETS):
            out = jnp.where(bkt == b, (table_ref[b, h] - far) * unit, out)
        return out

    for h in range(N_HEADS):
        for kind in range(2):
            bias_p_ref[h, kind] = lookup(bkt_p_ref[kind], h, LOG2E)
        bias_sp_ref[h * 8:(h + 1) * 8, :] = lookup(bkt_sp_ref[...], h, 1.0)
        bias_sn_ref[h * 8:(h + 1) * 8, :] = lookup(bkt_sn_ref[...], h, 1.0)

    rows = lax.broadcasted_iota(jnp.int32, (SUBLANES, LANES), 0)
    lam = jnp.zeros((SUBLANES, LANES), F32)
    for l in range(depth):
        lam_init = 0.8 - 0.6 * math.exp(-0.3 * l)
        s1 = jnp.sum(lq1_ref[l:l + 1, :] * lk1_ref[l:l + 1, :], axis=1, keepdims=True)
        s2 = jnp.sum(lq2_ref[l:l + 1, :] * lk2_ref[l:l + 1, :], axis=1, keepdims=True)
        val = jnp.exp(s1) - jnp.exp(s2) + lam_init
        lam = jnp.where(rows == l, val, lam)
    lam_ref[...] = lam


def _prep(table, lq1, lk1, lq2, lk2, t, depth):
    qry = np.arange(t)[:, None]
    key = np.arange(t)[None, :]
    bkt_p = np.stack([_bucket_np(qry - key), _bucket_np(t + qry - key)])
    tq = np.tile(np.arange(4), 2)[:, None]
    bkt_sp = _bucket_np(PAGE + tq - np.arange(PAGE)[None, :])
    cn = np.arange(LANES)[None, :]
    bkt_sn = np.where(cn < 4, _bucket_np(tq - cn), -1).astype(np.int32)
    vm = pl.BlockSpec(memory_space=pltpu.VMEM)
    return pl.pallas_call(
        functools.partial(_prep_kernel, depth=depth),
        out_shape=(jax.ShapeDtypeStruct((N_HEADS, 2, t, t), F32),
                   jax.ShapeDtypeStruct((N_HEADS * 8, PAGE), F32),
                   jax.ShapeDtypeStruct((N_HEADS * 8, LANES), F32),
                   jax.ShapeDtypeStruct((SUBLANES, LANES), F32)),
        in_specs=[pl.BlockSpec(memory_space=pltpu.SMEM), vm, vm, vm, vm, vm, vm, vm],
        out_specs=(vm, vm, vm, vm),
        compiler_params=pltpu.CompilerParams(vmem_limit_bytes=VMEM_LIMIT),
        name="prep",
    )(table, jnp.asarray(bkt_p), jnp.asarray(bkt_sp), jnp.asarray(bkt_sn), lq1, lk1, lq2, lk2)


def _inproj_kernel(x_ref, w_ref, wbd_ref, *refs, mix, tm, prompt, n_prev):
    if prompt:
        wt_ref = refs[0]
        qs_ref, k_ref, kt_ref, v_ref, vb_ref, ag_ref, pre_ref, dg_ref, bd_ref = refs[1 + n_prev:]
    else:
        q_ref, k_ref, v_ref, ag_ref, pre_ref, dg_ref, bd_ref = refs
    xb = x_ref[...].astype(BF16)

    def seg(i, n=1):
        return _dot(xb, w_ref[:, i * mix:(i + n) * mix])

    scale = QK_DIM ** -0.5
    k = seg(1)
    v = seg(2)
    if prompt:
        q = seg(0) * (scale * LOG2E)
        lane = lax.broadcasted_iota(jnp.int32, q.shape, 1)
        first = (lane % HEAD_DIM) < QK_DIM
        qs_ref[0] = jnp.where(first, q, 0.0).astype(BF16)
        qs_ref[1] = jnp.where(first, 0.0, q).astype(BF16)
        kt_ref[...] = _dot(wt_ref[...], xb, NT).astype(BF16)
        vb_ref[...] = v.astype(BF16)
        for h in range(N_HEADS):
            k_ref[pl.ds(h, tm, stride=N_HEADS), :] = k[:, h * HEAD_DIM:(h + 1) * HEAD_DIM]
            v_ref[pl.ds(h, tm, stride=N_HEADS), :] = v[:, h * HEAD_DIM:(h + 1) * HEAD_DIM]
    else:
        q_ref[...] = seg(0) * scale
        k_ref[...] = k
        v_ref[...] = v
    ag_ref[...] = _silu(seg(3)).astype(ag_ref.dtype)
    pre_ref[...] = seg(4, 3)
    dg_ref[...] = _silu(seg(7)).astype(dg_ref.dtype)
    bd_ref[...] = _dot(xb, wbd_ref[...])


def _inproj(x, w_main, w_bd, w_t=None, kv_prev=None, *, tm, layer=0, depth=1):
    m, d = x.shape
    mix = d // 2
    prompt = w_t is not None
    row = lambda i: (i, 0)
    col = lambda i: (0, i)
    full = lambda i: (0, 0)
    o_mix = pl.BlockSpec((tm, mix), row)
    f_mix = jax.ShapeDtypeStruct((m, mix), F32)
    b_mix = jax.ShapeDtypeStruct((m, mix), BF16)
    in_specs = [pl.BlockSpec((tm, d), row), pl.BlockSpec(w_main.shape, full), pl.BlockSpec(w_bd.shape, full)]
    args = [x, w_main, w_bd]
    aliases = {}
    n_prev = 0
    if prompt:
        in_specs.append(pl.BlockSpec(w_t.shape, full))
        args.append(w_t)
        if kv_prev is not None:
            n_prev = 2
            in_specs += [pl.BlockSpec(memory_space=pl.ANY)] * 2
            args += list(kv_prev)
            aliases = {4: 1, 5: 3}
        kv_all = jax.ShapeDtypeStruct((depth, m * N_HEADS, HEAD_DIM), F32)
        kv_spec = pl.BlockSpec((None, tm * N_HEADS, HEAD_DIM), lambda i: (layer, i, 0))
        shapes = [jax.ShapeDtypeStruct((2, m, mix), BF16), kv_all, jax.ShapeDtypeStruct((mix, m), BF16),
                  kv_all, b_mix]
        specs = [pl.BlockSpec((2, tm, mix), lambda i: (0, i, 0)), kv_spec, pl.BlockSpec((mix, tm), col),
                 kv_spec, o_mix]
    else:
        shapes = [f_mix] * 3
        specs = [o_mix] * 3
    gate = b_mix if prompt else f_mix
    shapes += [gate, jax.ShapeDtypeStruct((m, 3 * mix), F32), gate, jax.ShapeDtypeStruct((m, LANES), F32)]
    specs += [o_mix, pl.BlockSpec((tm, 3 * mix), row), o_mix, pl.BlockSpec((tm, LANES), row)]
    return pl.pallas_call(
        functools.partial(_inproj_kernel, mix=mix, tm=tm, prompt=prompt, n_prev=n_prev),
        out_shape=tuple(shapes),
        grid=(m // tm,),
        in_specs=in_specs,
        out_specs=tuple(specs),
        input_output_aliases=aliases,
        compiler_params=_cparams(("parallel",)),
        name="inproj",
    )(*args)


def _head_norm_gate(o, gain, gate, scale):
    ms = jnp.mean(o * o, axis=-1, keepdims=True)
    return o * lax.rsqrt(ms + NORM_EPS) * (gain * scale) * gate


def _attn_prompt_kernel(ii_ref, jj_ref, qs_ref, kt_ref, vb_ref, bias_ref, ag_ref, lam_ref, gain_ref,
                        o_ref, m_ref, acc_ref, *, t, rw, layer, out_scale):
    s_idx = pl.program_id(1)
    i = ii_ref[s_idx]
    j = jj_ref[s_idx]
    units = [(h, c) for h in range(N_HEADS) for c in range(2 * t // rw)]

    @pl.when(j == 0)
    def _():
        m_ref[...] = jnp.full(m_ref.shape, NEG_INF, F32)
        acc_ref[...] = jnp.zeros(acc_ref.shape, F32)

    def scores(u, kind):
        h, c = u
        r0 = c * rw
        cols = slice(h * HEAD_DIM, (h + 1) * HEAD_DIM)
        if rw == 2 * t:
            q = qs_ref[:, :, cols].reshape(2 * t, HEAD_DIM)
        else:
            q = qs_ref[r0 // t, r0 % t:r0 % t + rw, cols]
        s = _dot(q, kt_ref[cols, :])
        if kind is not None and rw == 2 * t:
            s = s + jnp.concatenate([bias_ref[h, kind]] * 2, axis=0)
        elif kind is not None:
            s = s + bias_ref[h, kind, r0 % t:r0 % t + rw, :]
        return s

    def update(kind):
        ones = jnp.ones((t, LANES), BF16)
        v_aug = [jnp.concatenate([vb_ref[:, h * HEAD_DIM:(h + 1) * HEAD_DIM], ones], axis=1)
                 for h in range(N_HEADS)]
        s_next = scores(units[0], kind)
        for n, u in enumerate(units):
            h, c = u
            s = s_next
            if n + 1 < len(units):
                s_next = scores(units[n + 1], kind)
            rows = slice(c * rw, (c + 1) * rw)
            m_prev = m_ref[h, rows, :]
            m_new = jnp.maximum(m_prev, jnp.max(s, axis=1, keepdims=True))
            alpha = jnp.exp2(m_prev - m_new)
            p = jnp.exp2(s - jnp.concatenate([m_new] * (t // LANES), axis=1)).astype(BF16)
            acc_ref[h, rows, :] = (jnp.concatenate([alpha, alpha], axis=1) * acc_ref[h, rows, :]
                                   + _dot(p, v_aug[h]))
            m_ref[h, rows, :] = m_new

    @pl.when(j < i - 1)
    def _():
        update(None)

    @pl.when(j == i - 1)
    def _():
        update(1)

    @pl.when(j == i)
    def _():
        update(0)
        lam = lam_ref[layer:layer + 1, :]
        for h in range(N_HEADS):
            cols = slice(h * HEAD_DIM, (h + 1) * HEAD_DIM)
            o = acc_ref[h, :, :HEAD_DIM] / acc_ref[h, :, HEAD_DIM:]
            od = o[:t] - lam * o[t:]
            o_ref[:, cols] = _head_norm_gate(od, gain_ref[...], ag_ref[:, cols].astype(F32),
                                             out_scale).astype(o_ref.dtype)


def _attn_prompt(qs, kt, vb, bias_p, ag, lam, gain, *, batch, t, layer):
    m, mix = vb.shape
    n_t = m // batch // t
    ii = np.concatenate([np.full(i + 1, i) for i in range(n_t)]).astype(np.int32)
    jj = np.concatenate([np.arange(i + 1) for i in range(n_t)]).astype(np.int32)
    lam_init = 0.8 - 0.6 * math.exp(-0.3 * layer)
    rw = 2 * t
    q_rows = lambda b, s, ii, jj: (b * n_t + ii[s], 0)
    k_rows = lambda b, s, ii, jj: (b * n_t + jj[s], 0)
    k_cols = lambda b, s, ii, jj: (0, b * n_t + jj[s])
    const2 = lambda b, s, ii, jj: (0, 0)
    grid_spec = pltpu.PrefetchScalarGridSpec(
        num_scalar_prefetch=2,
        grid=(batch, len(ii)),
        in_specs=[pl.BlockSpec((2, t, mix), lambda b, s, ii, jj: (0, b * n_t + ii[s], 0)),
                  pl.BlockSpec((mix, t), k_cols),
                  pl.BlockSpec((t, mix), k_rows),
                  pl.BlockSpec(memory_space=pltpu.VMEM),
                  pl.BlockSpec((t, mix), q_rows),
                  pl.BlockSpec((SUBLANES, LANES), const2),
                  pl.BlockSpec((1, HEAD_DIM), const2)],
        out_specs=pl.BlockSpec((t, mix), q_rows),
        scratch_shapes=[pltpu.VMEM((N_HEADS, 2 * t, LANES), F32),
                        pltpu.VMEM((N_HEADS, 2 * t, HEAD_DIM + LANES), F32)],
    )
    return pl.pallas_call(
        functools.partial(_attn_prompt_kernel, t=t, rw=rw, layer=layer, out_scale=1.0 - lam_init),
        out_shape=jax.ShapeDtypeStruct((m, mix), BF16),
        grid_spec=grid_spec,
        compiler_params=_cparams(("parallel", "arbitrary")),
        name="attn_prompt",
    )(jnp.asarray(ii), jnp.asarray(jj), qs, kt, vb, bias_p, ag, lam, gain)


def _attn_sample_kernel(*refs, g, layer, out_scale):
    pt_ref = refs[0]
    q_ref, kn_ref, vn_ref = refs[1:4]
    k_refs = refs[4:4 + g]
    v_refs = refs[4 + g:4 + 2 * g]
    bias_sp_ref, bias_sn_ref, ag_ref, lam_ref, gain_ref = refs[4 + 2 * g:9 + 2 * g]
    o_ref = refs[9 + 2 * g]
    wq_ref, m_ref, l_ref, acc_ref, own_ref = refs[10 + 2 * g:]
    del pt_ref
    step = pl.program_id(1)
    n_steps = pl.num_programs(1)
    rows = N_HEADS * 8
    width = N_HEADS * HEAD_DIM

    @pl.when(step == 0)
    def _():
        r = lax.broadcasted_iota(jnp.int32, (rows, width), 0)
        c = lax.broadcasted_iota(jnp.int32, (rows, width), 1)
        keep = (c // HEAD_DIM == r // 8) & ((c % HEAD_DIM) // QK_DIM == (r // 4) % 2)
        wq = jnp.zeros((rows, width), F32)
        for tq in range(4):
            wq = jnp.where(keep & (r % 4 == tq), q_ref[tq:tq + 1, :], wq)
        wq_ref[...] = wq.astype(BF16)
        m_ref[...] = jnp.full(m_ref.shape, NEG_INF, F32)
        l_ref[...] = jnp.zeros(l_ref.shape, F32)
        acc_ref[...] = jnp.zeros(acc_ref.shape, F32)
        own_ref[...] = jnp.zeros(own_ref.shape, F32)

    def update(s, v_list):
        m_prev = m_ref[...]
        m_new = jnp.maximum(m_prev, jnp.max(s, axis=1, keepdims=True))
        alpha = jnp.exp(m_prev - m_new)
        p = jnp.exp(s - m_new)
        l_ref[...] = alpha * l_ref[...] + jnp.sum(p, axis=1, keepdims=True)
        p = p.astype(BF16)
        pv = None
        for idx, vb in enumerate(v_list):
            w = vb.shape[0]
            part = _dot(p[:, idx * PAGE:idx * PAGE + w] if w == PAGE else p[:, :w], vb)
            pv = part if pv is None else pv + part
        acc_ref[...] = alpha * acc_ref[...] + pv
        m_ref[...] = m_new

    wq = wq_ref[...]

    def page(ref):
        return jnp.concatenate([ref[pl.ds(h, PAGE, stride=N_HEADS), :] for h in range(N_HEADS)],
                               axis=1).astype(BF16)

    def page_scores():
        return [_dot(wq, page(k_refs[idx]), NT) for idx in range(g)]

    @pl.when(step < n_steps - 1)
    def _():
        update(jnp.concatenate(page_scores(), axis=1), [page(v_refs[idx]) for idx in range(g)])

    @pl.when(step == n_steps - 1)
    def _():
        sc = page_scores()
        sc[g - 1] = sc[g - 1] + bias_sp_ref[...]
        update(jnp.concatenate(sc, axis=1), [page(v_refs[idx]) for idx in range(g)])
        own_ref[0, 0:4, :] = kn_ref[...]
        own_ref[1, 0:4, :] = vn_ref[...]
        update(_dot(wq, own_ref[0].astype(BF16), NT) + bias_sn_ref[...], [own_ref[1].astype(BF16)])
        o = acc_ref[...] / l_ref[...]
        lam = lam_ref[layer:layer + 1, :]
        for h in range(N_HEADS):
            blk = o[h * 8:(h + 1) * 8, h * HEAD_DIM:(h + 1) * HEAD_DIM]
            od = (blk - lam * pltpu.roll(blk, 4, axis=0))[0:4]
            gate = ag_ref[:, h * HEAD_DIM:(h + 1) * HEAD_DIM].astype(F32)
            o_ref[:, h * HEAD_DIM:(h + 1) * HEAD_DIM] = _head_norm_gate(
                od, gain_ref[...], gate, out_scale).astype(o_ref.dtype)


def _attn_sample(q, k_new, v_new, cache_k, cache_v, page_table, bias_sp, bias_sn, ag, lam, gain, *, layer):
    bsz, tq, width = q.shape
    n_pages = page_table.shape[1]
    g = math.gcd(n_pages, 32)
    lam_init = 0.8 - 0.6 * math.exp(-0.3 * layer)
    per_b = lambda b, s, pt: (b, 0, 0)
    const2 = lambda b, s, pt: (0, 0)

    def page_spec(idx):
        return pl.BlockSpec((None, None, PAGE * N_HEADS, HEAD_DIM),
                            lambda b, s, pt: (layer, pt[b, s * g + idx], 0, 0))

    grid_spec = pltpu.PrefetchScalarGridSpec(
        num_scalar_prefetch=1,
        grid=(bsz, n_pages // g),
        in_specs=([pl.BlockSpec((None, tq, width), per_b)] * 3
                  + [page_spec(idx) for idx in range(g)] * 2
                  + [pl.BlockSpec(bias_sp.shape, const2), pl.BlockSpec(bias_sn.shape, const2),
                     pl.BlockSpec((None, tq, width), per_b),
                     pl.BlockSpec((SUBLANES, LANES), const2), pl.BlockSpec((1, HEAD_DIM), const2)]),
        out_specs=pl.BlockSpec((None, tq, width), per_b),
        scratch_shapes=[pltpu.VMEM((N_HEADS * 8, width), BF16), pltpu.VMEM((N_HEADS * 8, 1), F32),
                        pltpu.VMEM((N_HEADS * 8, 1), F32), pltpu.VMEM((N_HEADS * 8, width), F32),
                        pltpu.VMEM((2, PAGE, width), F32)],
    )
    return pl.pallas_call(
        functools.partial(_attn_sample_kernel, g=g, layer=layer, out_scale=1.0 - lam_init),
        out_shape=jax.ShapeDtypeStruct((bsz, tq, width), F32),
        grid_spec=grid_spec,
        compiler_params=_cparams(("parallel", "arbitrary")),
        name="attn_sample",
    )(page_table, q, k_new, v_new, *([cache_k] * g), *([cache_v] * g), bias_sp, bias_sn, ag, lam, gain)


def _unit_lower_inverses(lmats, c):
    r = lax.broadcasted_iota(jnp.int32, (c, c), 0)
    q = lax.broadcasted_iota(jnp.int32, (c, c), 1)
    eye = (r == q).astype(F32)
    base = min(c, 16)
    same_base = r // base == q // base
    zero = jnp.zeros((c, c), BF16)

    def masked(parts, mask, sign):
        return tuple(jnp.where(mask, sign * part, zero) for part in parts)

    l_parts = [_split(lm) for lm in lmats]
    p_parts = [masked(lp, same_base, -1) for lp in l_parts]
    xs = [eye + jnp.where(same_base, -lm, 0.0) for lm in lmats]
    for _ in range(int(math.log2(base)) - 1):
        p_parts = [_split(_dot3s(pp, pp)) for pp in p_parts]
        xs = [x + _dot3s(_split(x), pp) for x, pp in zip(xs, p_parts)]
    x_parts = [_split(x) for x in xs]
    s = base
    while s < c:
        below = (r // (2 * s) == q // (2 * s)) & (r // s != q // s)
        t_parts = [_split(_dot3s(masked(lp, below, 1), xp)) for lp, xp in zip(l_parts, x_parts)]
        xs = [x - _dot3s(xp, tp) for x, xp, tp in zip(xs, x_parts, t_parts)]
        x_parts = [_split(x) for x in xs]
        s *= 2
    return x_parts


def _delta_kernel(pre_ref, bd_ref, dg_ref, buf_ref, cw_ref, hp_ref, s0_ref, gain_ref,
                  o_ref, s_out_ref, conv_out_ref, s_ref, xcat_ref, *, nb, c, mix, t_valid):
    n = pl.program_id(1)
    n_chunks = pl.num_programs(1)
    units = [(b, h) for b in range(nb) for h in range(N_HEADS)]

    @pl.when(n == 0)
    def _():
        s_ref[...] = s0_ref[...]
        xcat_ref[:, 0:8, :] = buf_ref[...]

    r = lax.broadcasted_iota(jnp.int32, (c, c), 0)
    q_i = lax.broadcasted_iota(jnp.int32, (c, c), 1)
    incl = r >= q_i
    strict = r > q_i
    incl_f = incl.astype(F32)
    sel = (lax.broadcasted_iota(jnp.int32, (8, LANES), 1)
           == lax.broadcasted_iota(jnp.int32, (8, LANES), 0) + 4).astype(F32)
    eye_h = (lax.broadcasted_iota(jnp.int32, (HEAD_DIM, HEAD_DIM), 0)
             == lax.broadcasted_iota(jnp.int32, (HEAD_DIM, HEAD_DIM), 1)).astype(BF16)
    last_valid = t_valid - (t_valid - 1) // c * c

    ys, beta_ts, gcum_ts, gcum_rows = [], [], [], []
    for b in range(nb):
        xcat_ref[b, 8:8 + c, :] = pre_ref[b]
        y = xcat_ref[b, 5:5 + c, :] * cw_ref[0:1, :]
        for j in range(1, CONV_W):
            y = y + xcat_ref[b, 5 + j:5 + j + c, :] * cw_ref[j:j + 1, :]
        ys.append(_silu(y))
        conv_out_ref[b] = xcat_ref[b, last_valid:last_valid + 8, :]
        xcat_ref[b, 0:8, :] = xcat_ref[b, c:c + 8, :]

        bd = bd_ref[b]
        beta_t = _sigmoid(bd)
        g_t = -jnp.exp(hp_ref[0:1, :]) * _softplus(bd + hp_ref[1:2, :])
        if t_valid % c:
            row_ok = (lax.broadcasted_iota(jnp.int32, bd.shape, 0) + n * c) < t_valid
            beta_t = jnp.where(row_ok, beta_t, 0.0)
            g_t = jnp.where(row_ok, g_t, 0.0)
        beta_ts.append(beta_t)
        gcum_ts.append(_dot2(incl_f, g_t))
    for b in range(nb):
        gcum_rows.append(_dot2(sel, gcum_ts[b], NT))

    qn, kn, vb_, kb, e_g, decay, g_last, gcum = {}, {}, {}, {}, {}, {}, {}, {}
    for u in units:
        b, h = u
        lo, hi = h * HEAD_DIM, (h + 1) * HEAD_DIM
        qh = ys[b][:, lo:hi]
        kh = ys[b][:, mix + lo:mix + hi]
        qn[u] = qh * lax.rsqrt(jnp.sum(qh * qh, axis=-1, keepdims=True) + NORM_EPS) * (HEAD_DIM ** -0.5)
        kn[u] = kh * lax.rsqrt(jnp.sum(kh * kh, axis=-1, keepdims=True) + NORM_EPS)
        beta = beta_ts[b][:, h:h + 1]
        gcum[u] = gcum_ts[b][:, 4 + h:5 + h]
        g_last[u] = gcum_ts[b][c - 1:c, 4 + h:5 + h]
        diff = gcum[u] - gcum_rows[b][h:h + 1, :]
        decay[u] = jnp.where(incl, jnp.exp(jnp.where(incl, diff, 0.0)), 0.0)
        kb[u] = kn[u] * beta
        vb_[u] = ys[b][:, 2 * mix + lo:2 * mix + hi] * beta
        e_g[u] = jnp.exp(gcum[u])
    knb = {u: kn[u].astype(BF16) for u in units}
    lmat = {u: jnp.where(strict, _dot(kb[u].astype(BF16), knb[u], NT) * decay[u], 0.0) for u in units}
    a_qk = {u: jnp.where(incl, _dot(qn[u].astype(BF16), knb[u], NT) * decay[u], 0.0).astype(BF16)
            for u in units}

    tinv = dict(zip(units, _unit_lower_inverses([lmat[u] for u in units], c)))
    uw = {u: _dot3s(tinv[u], _split(jnp.concatenate([vb_[u], kb[u] * e_g[u]], axis=1))) for u in units}

    s_old = {u: s_ref[u[0], u[1]] for u in units}
    s_b = {u: s_old[u].astype(BF16) for u in units}
    ws = {u: _dot(uw[u][:, HEAD_DIM:].astype(BF16), s_b[u]) for u in units}
    qs = {u: _dot((qn[u] * e_g[u]).astype(BF16), s_b[u]) for u in units}
    k_tail_t = {u: _dot(eye_h, (kn[u] * jnp.exp(g_last[u] - gcum[u])).astype(BF16), NT).astype(BF16)
                for u in units}
    vnb = {u: (uw[u][:, :HEAD_DIM] - ws[u]).astype(BF16) for u in units}
    o = {u: qs[u] + _dot(a_qk[u], vnb[u]) for u in units}
    for u in units:
        s_ref[u[0], u[1]] = s_old[u] * jnp.exp(g_last[u]) + _dot(k_tail_t[u], vnb[u])
    for u in units:
        b, h = u
        lo, hi = h * HEAD_DIM, (h + 1) * HEAD_DIM
        ms = jnp.mean(o[u] * o[u], axis=-1, keepdims=True)
        o_n = o[u] * lax.rsqrt(ms + NORM_EPS) * gain_ref[...]
        o_ref[b, :, lo:hi] = (o_n * dg_ref[b, :, lo:hi].astype(F32)).astype(o_ref.dtype)

    @pl.when(n == n_chunks - 1)
    def _():
        s_out_ref[...] = s_ref[...]


def _delta(pre, bd, dg, conv_buf, conv_w, hp, s0, gain, *, nb, c, t_valid):
    batch, t_pad, width3 = pre.shape
    mix = width3 // 3
    n_chunks = t_pad // c
    tok = lambda b, n: (b, n, 0)
    per_b3 = lambda b, n: (b, 0, 0)
    per_b4 = lambda b, n: (b, 0, 0, 0)
    const2 = lambda b, n: (0, 0)
    o_d, s_new, conv_out = pl.pallas_call(
        functools.partial(_delta_kernel, nb=nb, c=c, mix=mix, t_valid=t_valid),
        out_shape=(jax.ShapeDtypeStruct((batch, t_pad, mix), dg.dtype),
                   jax.ShapeDtypeStruct(s0.shape, F32),
                   jax.ShapeDtypeStruct((batch, 8, width3), F32)),
        grid=(batch // nb, n_chunks),
        in_specs=[pl.BlockSpec((nb, c, width3), tok), pl.BlockSpec((nb, c, LANES), tok),
                  pl.BlockSpec((nb, c, mix), tok),
                  pl.BlockSpec((nb, 8, width3), per_b3),
                  pl.BlockSpec((8, width3), const2),
                  pl.BlockSpec((8, LANES), const2),
                  pl.BlockSpec((nb, N_HEADS, HEAD_DIM, HEAD_DIM), per_b4),
                  pl.BlockSpec((1, HEAD_DIM), const2)],
        out_specs=(pl.BlockSpec((nb, c, mix), tok),
                   pl.BlockSpec((nb, N_HEADS, HEAD_DIM, HEAD_DIM), per_b4),
                   pl.BlockSpec((nb, 8, width3), per_b3)),
        scratch_shapes=[pltpu.VMEM((nb, N_HEADS, HEAD_DIM, HEAD_DIM), F32),
                        pltpu.VMEM((nb, c + 8, width3), F32)],
        compiler_params=_cparams(("parallel", "arbitrary")),
        name="delta",
    )(pre, bd, dg, conv_buf, conv_w, hp, s0, gain)
    return o_d, s_new, conv_out[:, 5:8, :]


def _post_kernel(x_ref, oa_ref, od_ref, wmg_ref, wbr_ref, wo_ref, lng_ref, lnb_ref, y_ref, *, d, alpha):
    x = x_ref[...]
    gates = _sigmoid(_dot(x.astype(BF16), wmg_ref[...]))
    h_a = _dot(oa_ref[...].astype(BF16), wbr_ref[0])
    h_d = _dot(od_ref[...].astype(BF16), wbr_ref[1])
    merged = gates[:, :d] * h_a + gates[:, d:] * h_d
    y = _dot(merged.astype(BF16), wo_ref[...])
    z = alpha * x + y
    mu = jnp.mean(z, axis=-1, keepdims=True)
    zc = z - mu
    var = jnp.mean(zc * zc, axis=-1, keepdims=True)
    y_ref[...] = zc * lax.rsqrt(var + LN_EPS) * lng_ref[...] + lnb_ref[...]


def _post(x, o_a, o_d, w_mg, w_br, w_o, ln_g, ln_b, *, alpha):
    m, d = x.shape
    tm = min(512, m)
    row = lambda i: (i, 0)
    c2 = lambda i: (0, 0)
    return pl.pallas_call(
        functools.partial(_post_kernel, d=d, alpha=alpha),
        out_shape=jax.ShapeDtypeStruct((m, d), F32),
        grid=(m // tm,),
        in_specs=[pl.BlockSpec((tm, d), row), pl.BlockSpec((tm, d // 2), row), pl.BlockSpec((tm, d // 2), row),
                  pl.BlockSpec(w_mg.shape, c2), pl.BlockSpec(w_br.shape, lambda i: (0, 0, 0)),
                  pl.BlockSpec(w_o.shape, c2), pl.BlockSpec((1, d), c2), pl.BlockSpec((1, d), c2)],
        out_specs=pl.BlockSpec((tm, d), row),
        compiler_params=_cparams(("parallel",)),
        name="post",
    )(x, o_a, o_d, w_mg, w_br, w_o, ln_g, ln_b)


def kernel(x_prompt, x_sample, cache_k, cache_v, state_delta, state_conv, page_table, rel_bias_table,
           w_in, conv_w, lambda_q1, lambda_k1, lambda_q2, lambda_k2, diff_norm_g, a_log, dt_bias,
           delta_norm_g, w_branch, w_o, ln_g, ln_b):
    batch, seq, d = x_prompt.shape
    dec_b, dec_t, _ = x_sample.shape
    depth = w_in.shape[0]
    mix = d // 2
    assert mix == N_HEADS * HEAD_DIM and dec_t == 4 and cache_k.shape[2] == PAGE
    alpha = (2 * depth) ** 0.25
    t_attn = min(512, seq)
    c_prompt = min(64, seq)
    c_sample = 8
    nb_prompt = math.gcd(batch, 2)
    nb_sample = math.gcd(dec_b, 2)
    n_pool = cache_k.shape[1]

    bias_p, bias_sp, bias_sn, lam = _prep(rel_bias_table, lambda_q1, lambda_k1, lambda_q2, lambda_k2,
                                          t_attn, depth)
    cache_k2 = cache_k.reshape(depth, n_pool, PAGE * N_HEADS, HEAD_DIM)
    cache_v2 = cache_v.reshape(depth, n_pool, PAGE * N_HEADS, HEAD_DIM)

    hp = x_prompt.reshape(batch * seq, d)
    hs = x_sample.reshape(dec_b * dec_t, d)
    outs = {k: [] for k in ("ks", "vs", "dp", "ds", "cp", "cs")}
    k_all = jnp.zeros((depth, batch * seq * N_HEADS, HEAD_DIM), F32)
    v_all = jnp.zeros((depth, batch * seq * N_HEADS, HEAD_DIM), F32)
    zeros_buf = jnp.zeros((batch, 8, 3 * mix), F32)
    zeros_state = jnp.zeros((batch, N_HEADS, HEAD_DIM, HEAD_DIM), F32)

    for l in range(depth):
        w_main = w_in[l, :, :8 * mix].astype(BF16)
        w_bd = jnp.pad(w_in[l, :, 8 * mix:8 * mix + 2 * N_HEADS], ((0, 0), (0, LANES - 2 * N_HEADS))).astype(BF16)
        w_mg = w_in[l, :, 8 * mix + 2 * N_HEADS:].astype(BF16)
        w_kt = w_in[l, :, mix:2 * mix].T.astype(BF16)
        w_br = w_branch[l].astype(BF16)
        w_ol = w_o[l].astype(BF16)
        cw = jnp.pad(conv_w[l], ((0, 8 - CONV_W), (0, 0)))
        hpar = jnp.zeros((8, LANES), F32)
        hpar = hpar.at[0, N_HEADS:2 * N_HEADS].set(a_log[l]).at[1, N_HEADS:2 * N_HEADS].set(dt_bias[l])
        gain_a = diff_norm_g[l][None, :]
        gain_d = delta_norm_g[l][None, :]
        lng = ln_g[l][None, :]
        lnb = ln_b[l][None, :]

        qs, k_all, kt, v_all, vb, ag, pre, dg, bd = _inproj(hp, w_main, w_bd, w_kt, (k_all, v_all),
                                                            tm=t_attn, layer=l, depth=depth)
        o_a = _attn_prompt(qs, kt, vb, bias_p, ag, lam, gain_a, batch=batch, t=t_attn, layer=l)
        o_d, s_new, c_new = _delta(pre.reshape(batch, seq, 3 * mix), bd.reshape(batch, seq, LANES),
                                   dg.reshape(batch, seq, mix), zeros_buf, cw, hpar, zeros_state, gain_d,
                                   nb=nb_prompt, c=c_prompt, t_valid=seq)
        hp = _post(hp, o_a, o_d.reshape(batch * seq, mix), w_mg, w_br, w_ol, lng, lnb, alpha=alpha)
        outs["dp"].append(s_new)
        outs["cp"].append(c_new)

        q_s, k_s, v_s, ag_s, pre_s, dg_s, bd_s = _inproj(hs, w_main, w_bd, tm=dec_b * dec_t)
        o_a = _attn_sample(q_s.reshape(dec_b, dec_t, mix), k_s.reshape(dec_b, dec_t, mix),
                           v_s.reshape(dec_b, dec_t, mix), cache_k2, cache_v2, page_table,
                           bias_sp, bias_sn, ag_s.reshape(dec_b, dec_t, mix), lam, gain_a, layer=l)

        def pad_rows(a):
            a = a.reshape(dec_b, dec_t, a.shape[-1])
            return jnp.pad(a, ((0, 0), (0, c_sample - dec_t), (0, 0)))

        buf_s = jnp.pad(state_conv[l], ((0, 0), (8 - (CONV_W - 1), 0), (0, 0)))
        o_d, s_new, c_new = _delta(pad_rows(pre_s), pad_rows(bd_s), pad_rows(dg_s), buf_s, cw, hpar,
                                   state_delta[l], gain_d, nb=nb_sample, c=c_sample, t_valid=dec_t)
        o_d = o_d[:, :dec_t].reshape(dec_b * dec_t, mix)
        hs = _post(hs, o_a.reshape(dec_b * dec_t, mix), o_d, w_mg, w_br, w_ol, lng, lnb, alpha=alpha)
        outs["ks"].append(k_s.reshape(dec_b, dec_t, N_HEADS, HEAD_DIM))
        outs["vs"].append(v_s.reshape(dec_b, dec_t, N_HEADS, HEAD_DIM))
        outs["ds"].append(s_new)
        outs["cs"].append(c_new)

    st = {k: jnp.stack(v) for k, v in outs.items()}
    kv_shape = (depth, batch, seq, N_HEADS, HEAD_DIM)
    return (hp.reshape(batch, seq, d), hs.reshape(dec_b, dec_t, d), k_all.reshape(kv_shape), v_all.reshape(kv_shape),
            st["ks"], st["vs"], st["dp"], st["ds"], st["cp"], st["cs"])
```

```python
import functools
import math

import jax
import jax.numpy as jnp
import numpy as np
from jax import lax
from jax.experimental import pallas as pl
from jax.experimental.pallas import tpu as pltpu

F32 = jnp.float32
BF16 = jnp.bfloat16

N_HEADS = 4
HEAD_DIM = 128
QK_DIM = 64
CONV_W = 4
PAGE = 128
N_BUCKETS = 32
MAX_DISTANCE = 128
LN_EPS = 1e-5
NORM_EPS = 1e-6
NEG_INF = -1e30
LOG2E = math.log2(math.e)

LANES = 128
SUBLANES = 8
VMEM_LIMIT = 56 * 1024 * 1024

NN = (((1,), (0,)), ((), ()))
NT = (((1,), (1,)), ((), ()))


def _dot(a, b, dims=NN):
    return lax.dot_general(a, b, dims, preferred_element_type=F32)


def _split(a):
    hi = a.astype(BF16)
    lo = (a - hi.astype(F32)).astype(BF16)
    return hi, lo


def _dot3s(a_parts, b_parts, dims=NN):
    ah, al = a_parts
    bh, bl = b_parts
    return _dot(ah, bh, dims) + (_dot(ah, bl, dims) + _dot(al, bh, dims))


def _dot2(a_exact, b, dims=NN):
    bh, bl = _split(b)
    a = a_exact.astype(BF16)
    return _dot(a, bh, dims) + _dot(a, bl, dims)


def _silu(x):
    return x * (1.0 / (1.0 + jnp.exp(-x)))


def _sigmoid(x):
    return 1.0 / (1.0 + jnp.exp(-x))


def _softplus(x):
    return jnp.maximum(x, 0.0) + jnp.log(1.0 + jnp.exp(-jnp.abs(x)))


def _cparams(sem):
    return pltpu.CompilerParams(dimension_semantics=sem, vmem_limit_bytes=VMEM_LIMIT)


def _bucket_np(dist):
    dist = np.asarray(dist, np.int64)
    n = np.maximum(dist, 0)
    max_exact = N_BUCKETS // 2
    nf = np.maximum(n, 1).astype(np.float32)
    large = max_exact + (np.log(nf / np.float32(max_exact)) / np.float32(math.log(MAX_DISTANCE / max_exact))
                         * np.float32(N_BUCKETS - max_exact)).astype(np.int32)
    large = np.minimum(large, N_BUCKETS - 1)
    b = np.where(n < max_exact, n, large)
    return np.where(dist < 0, -1, b).astype(np.int32)


def _prep_kernel(table_ref, bkt_p_ref, bkt_sp_ref, bkt_sn_ref, lq1_ref, lk1_ref, lq2_ref, lk2_ref,
                 bias_p_ref, bias_sp_ref, bias_sn_ref, lam_ref, *, depth):
    def lookup(bkt, h, unit):
        far = table_ref[N_BUCKETS - 1, h]
        out = jnp.full(bkt.shape, NEG_INF, F32)
        for b in range(N_BUCKETS):
            out = jnp.where(bkt == b, (table_ref[b, h] - far) * unit, out)
        return out

    for h in range(N_HEADS):
        for kind in range(2):
            bias_p_ref[h, kind] = lookup(bkt_p_ref[kind], h, LOG2E)
        bias_sp_ref[h * 8:(h + 1) * 8, :] = lookup(bkt_sp_ref[...], h, 1.0)
        bias_sn_ref[h * 8:(h + 1) * 8, :] = lookup(bkt_sn_ref[...], h, 1.0)

    rows = lax.broadcasted_iota(jnp.int32, (SUBLANES, LANES), 0)
    lam = jnp.zeros((SUBLANES, LANES), F32)
    for l in range(depth):
        lam_init = 0.8 - 0.6 * math.exp(-0.3 * l)
        s1 = jnp.sum(lq1_ref[l:l + 1, :] * lk1_ref[l:l + 1, :], axis=1, keepdims=True)
        s2 = jnp.sum(lq2_ref[l:l + 1, :] * lk2_ref[l:l + 1, :], axis=1, keepdims=True)
        val = jnp.exp(s1) - jnp.exp(s2) + lam_init
        lam = jnp.where(rows == l, val, lam)
    lam_ref[...] = lam


def _prep(table, lq1, lk1, lq2, lk2, t, depth):
    qry = np.arange(t)[:, None]
    key = np.arange(t)[None, :]
    bkt_p = np.stack([_bucket_np(qry - key), _bucket_np(t + qry - key)])
    tq = np.tile(np.arange(4), 2)[:, None]
    bkt_sp = _bucket_np(PAGE + tq - np.arange(PAGE)[None, :])
    cn = np.arange(LANES)[None, :]
    bkt_sn = np.where(cn < 4, _bucket_np(tq - cn), -1).astype(np.int32)
    vm = pl.BlockSpec(memory_space=pltpu.VMEM)
    return pl.pallas_call(
        functools.partial(_prep_kernel, depth=depth),
        out_shape=(jax.ShapeDtypeStruct((N_HEADS, 2, t, t), F32),
                   jax.ShapeDtypeStruct((N_HEADS * 8, PAGE), F32),
                   jax.ShapeDtypeStruct((N_HEADS * 8, LANES), F32),
                   jax.ShapeDtypeStruct((SUBLANES, LANES), F32)),
        in_specs=[pl.BlockSpec(memory_space=pltpu.SMEM), vm, vm, vm, vm, vm, vm, vm],
        out_specs=(vm, vm, vm, vm),
        compiler_params=pltpu.CompilerParams(vmem_limit_bytes=VMEM_LIMIT),
        name="prep",
    )(table, jnp.asarray(bkt_p), jnp.asarray(bkt_sp), jnp.asarray(bkt_sn), lq1, lk1, lq2, lk2)


def _inproj_kernel(x_ref, w_ref, wbd_ref, *refs, mix, tm, prompt, n_prev):
    if prompt:
        wt_ref = refs[0]
        qs_ref, k_ref, kt_ref, v_ref, vb_ref, ag_ref, pre_ref, dg_ref, bd_ref = refs[1 + n_prev:]
    else:
        q_ref, k_ref, v_ref, ag_ref, pre_ref, dg_ref, bd_ref = refs
    xb = x_ref[...].astype(BF16)

    def seg(i, n=1):
        return _dot(xb, w_ref[:, i * mix:(i + n) * mix])

    scale = QK_DIM ** -0.5
    k = seg(1)
    v = seg(2)
    if prompt:
        q = seg(0) * (scale * LOG2E)
        lane = lax.broadcasted_iota(jnp.int32, q.shape, 1)
        first = (lane % HEAD_DIM) < QK_DIM
        qs_ref[0] = jnp.where(first, q, 0.0).astype(BF16)
        qs_ref[1] = jnp.where(first, 0.0, q).astype(BF16)
        kt_ref[...] = _dot(wt_ref[...], xb, NT).astype(BF16)
        vb_ref[...] = v.astype(BF16)
        for h in range(N_HEADS):
            k_ref[pl.ds(h, tm, stride=N_HEADS), :] = k[:, h * HEAD_DIM:(h + 1) * HEAD_DIM]
            v_ref[pl.ds(h, tm, stride=N_HEADS), :] = v[:, h * HEAD_DIM:(h + 1) * HEAD_DIM]
    else:
        q_ref[...] = seg(0) * scale
        k_ref[...] = k
        v_ref[...] = v
    ag_ref[...] = _silu(seg(3)).astype(ag_ref.dtype)
    pre_ref[...] = seg(4, 3)
    dg_ref[...] = _silu(seg(7)).astype(dg_ref.dtype)
    bd_ref[...] = _dot(xb, wbd_ref[...])


def _inproj(x, w_main, w_bd, w_t=None, kv_prev=None, *, tm, layer=0, depth=1):
    m, d = x.shape
    mix = d // 2
    prompt = w_t is not None
    row = lambda i: (i, 0)
    col = lambda i: (0, i)
    full = lambda i: (0, 0)
    o_mix = pl.BlockSpec((tm, mix), row)
    f_mix = jax.ShapeDtypeStruct((m, mix), F32)
    b_mix = jax.ShapeDtypeStruct((m, mix), BF16)
    in_specs = [pl.BlockSpec((tm, d), row), pl.BlockSpec(w_main.shape, full), pl.BlockSpec(w_bd.shape, full)]
    args = [x, w_main, w_bd]
    aliases = {}
    n_prev = 0
    if prompt:
        in_specs.append(pl.BlockSpec(w_t.shape, full))
        args.append(w_t)
        if kv_prev is not None:
            n_prev = 2
            in_specs += [pl.BlockSpec(memory_space=pl.ANY)] * 2
            args += list(kv_prev)
            aliases = {4: 1, 5: 3}
        kv_all = jax.ShapeDtypeStruct((depth, m * N_HEADS, HEAD_DIM), F32)
        kv_spec = pl.BlockSpec((None, tm * N_HEADS, HEAD_DIM), lambda i: (layer, i, 0))
        shapes = [jax.ShapeDtypeStruct((2, m, mix), BF16), kv_all, jax.ShapeDtypeStruct((mix, m), BF16),
                  kv_all, b_mix]
        specs = [pl.BlockSpec((2, tm, mix), lambda i: (0, i, 0)), kv_spec, pl.BlockSpec((mix, tm), col),
                 kv_spec, o_mix]
    else:
        shapes = [f_mix] * 3
        specs = [o_mix] * 3
    gate = b_mix if prompt else f_mix
    shapes += [gate, jax.ShapeDtypeStruct((m, 3 * mix), F32), gate, jax.ShapeDtypeStruct((m, LANES), F32)]
    specs += [o_mix, pl.BlockSpec((tm, 3 * mix), row), o_mix, pl.BlockSpec((tm, LANES), row)]
    return pl.pallas_call(
        functools.partial(_inproj_kernel, mix=mix, tm=tm, prompt=prompt, n_prev=n_prev),
        out_shape=tuple(shapes),
        grid=(m // tm,),
        in_specs=in_specs,
        out_specs=tuple(specs),
        input_output_aliases=aliases,
        compiler_params=_cparams(("parallel",)),
        name="inproj",
    )(*args)


def _head_norm_gate(o, gain, gate, scale):
    ms = jnp.mean(o * o, axis=-1, keepdims=True)
    return o * lax.rsqrt(ms + NORM_EPS) * (gain * scale) * gate


def _attn_prompt_kernel(ii_ref, jj_ref, qs_ref, kt_ref, vb_ref, bias_ref, ag_ref, lam_ref, gain_ref,
                        o_ref, m_ref, acc_ref, *, t, rw, layer, out_scale):
    s_idx = pl.program_id(1)
    i = ii_ref[s_idx]
    j = jj_ref[s_idx]

    @pl.when(j == 0)
    def _():
        m_ref[...] = jnp.full(m_ref.shape, NEG_INF, F32)
        acc_ref[...] = jnp.zeros(acc_ref.shape, F32)

    def keys_seen(kind, r0, rw):
        return min(t, r0 % t + rw) if kind == 0 else t

    def scores(u, kind, rw):
        h, c = u
        r0 = c * rw
        nk = keys_seen(kind, r0, rw)
        cols = slice(h * HEAD_DIM, (h + 1) * HEAD_DIM)
        q = qs_ref[r0 // t, r0 % t:r0 % t + rw, cols]
        s = _dot(q, kt_ref[cols, :nk])
        if kind is not None:
            s = s + bias_ref[h, kind, r0 % t:r0 % t + rw, :nk]
        return s

    def update(kind, rw):
        units = [(h, c) for h in range(N_HEADS) for c in range(2 * t // rw)]
        ones = jnp.ones((t, LANES), BF16)
        v_aug = [jnp.concatenate([vb_ref[:, h * HEAD_DIM:(h + 1) * HEAD_DIM], ones], axis=1)
                 for h in range(N_HEADS)]
        s_next = scores(units[0], kind, rw)
        for n, u in enumerate(units):
            h, c = u
            s = s_next
            if n + 1 < len(units):
                s_next = scores(units[n + 1], kind, rw)
            nk = keys_seen(kind, c * rw, rw)
            rows = slice(c * rw, (c + 1) * rw)
            m_prev = m_ref[h, rows, :]
            m_new = jnp.maximum(m_prev, jnp.max(s, axis=1, keepdims=True))
            alpha = jnp.exp2(m_prev - m_new)
            p = jnp.exp2(s - jnp.concatenate([m_new] * (nk // LANES), axis=1)).astype(BF16)
            acc_ref[h, rows, :] = (jnp.concatenate([alpha, alpha], axis=1) * acc_ref[h, rows, :]
                                   + _dot(p, v_aug[h][:nk]))
            m_ref[h, rows, :] = m_new

    @pl.when(j < i - 1)
    def _():
        update(None, rw)

    @pl.when(j == i - 1)
    def _():
        update(1, rw)

    @pl.when(j == i)
    def _():
        update(0, t // 2)
        lam = lam_ref[layer:layer + 1, :]
        for h in range(N_HEADS):
            cols = slice(h * HEAD_DIM, (h + 1) * HEAD_DIM)
            o = acc_ref[h, :, :HEAD_DIM] / acc_ref[h, :, HEAD_DIM:]
            od = o[:t] - lam * o[t:]
            o_ref[:, cols] = _head_norm_gate(od, gain_ref[...], ag_ref[:, cols].astype(F32),
                                             out_scale).astype(o_ref.dtype)


def _attn_prompt(qs, kt, vb, bias_p, ag, lam, gain, *, batch, t, layer):
    m, mix = vb.shape
    n_t = m // batch // t
    ii = np.concatenate([np.full(i + 1, i) for i in range(n_t)]).astype(np.int32)
    jj = np.concatenate([np.arange(i + 1) for i in range(n_t)]).astype(np.int32)
    lam_init = 0.8 - 0.6 * math.exp(-0.3 * layer)
    rw = min(t, 4 * LANES)
    q_rows = lambda b, s, ii, jj: (b * n_t + ii[s], 0)
    k_rows = lambda b, s, ii, jj: (b * n_t + jj[s], 0)
    k_cols = lambda b, s, ii, jj: (0, b * n_t + jj[s])
    const2 = lambda b, s, ii, jj: (0, 0)
    grid_spec = pltpu.PrefetchScalarGridSpec(
        num_scalar_prefetch=2,
        grid=(batch, len(ii)),
        in_specs=[pl.BlockSpec((2, t, mix), lambda b, s, ii, jj: (0, b * n_t + ii[s], 0)),
                  pl.BlockSpec((mix, t), k_cols),
                  pl.BlockSpec((t, mix), k_rows),
                  pl.BlockSpec(memory_space=pltpu.VMEM),
                  pl.BlockSpec((t, mix), q_rows),
                  pl.BlockSpec((SUBLANES, LANES), const2),
                  pl.BlockSpec((1, HEAD_DIM), const2)],
        out_specs=pl.BlockSpec((t, mix), q_rows),
        scratch_shapes=[pltpu.VMEM((N_HEADS, 2 * t, LANES), F32),
                        pltpu.VMEM((N_HEADS, 2 * t, HEAD_DIM + LANES), F32)],
    )
    return pl.pallas_call(
        functools.partial(_attn_prompt_kernel, t=t, rw=rw, layer=layer, out_scale=1.0 - lam_init),
        out_shape=jax.ShapeDtypeStruct((m, mix), BF16),
        grid_spec=grid_spec,
        compiler_params=_cparams(("parallel", "arbitrary")),
        name="attn_prompt",
    )(jnp.asarray(ii), jnp.asarray(jj), qs, kt, vb, bias_p, ag, lam, gain)


def _attn_sample_kernel(*refs, g, layer, out_scale):
    pt_ref = refs[0]
    q_ref, kn_ref, vn_ref = refs[1:4]
    k_refs = refs[4:4 + g]
    v_refs = refs[4 + g:4 + 2 * g]
    bias_sp_ref, bias_sn_ref, ag_ref, lam_ref, gain_ref = refs[4 + 2 * g:9 + 2 * g]
    o_ref = refs[9 + 2 * g]
    wq_ref, m_ref, l_ref, acc_ref, own_ref = refs[10 + 2 * g:]
    del pt_ref
    step = pl.program_id(1)
    n_steps = pl.num_programs(1)
    rows = N_HEADS * 8
    width = N_HEADS * HEAD_DIM

    @pl.when(step == 0)
    def _():
        r = lax.broadcasted_iota(jnp.int32, (rows, width), 0)
        c = lax.broadcasted_iota(jnp.int32, (rows, width), 1)
        keep = (c // HEAD_DIM == r // 8) & ((c % HEAD_DIM) // QK_DIM == (r // 4) % 2)
        wq = jnp.zeros((rows, width), F32)
        for tq in range(4):
            wq = jnp.where(keep & (r % 4 == tq), q_ref[tq:tq + 1, :], wq)
        wq_ref[...] = wq.astype(BF16)
        m_ref[...] = jnp.full(m_ref.shape, NEG_INF, F32)
        l_ref[...] = jnp.zeros(l_ref.shape, F32)
        acc_ref[...] = jnp.zeros(acc_ref.shape, F32)
        own_ref[...] = jnp.zeros(own_ref.shape, F32)

    def update(s, v_list):
        m_prev = m_ref[...]
        m_new = jnp.maximum(m_prev, jnp.max(s, axis=1, keepdims=True))
        alpha = jnp.exp(m_prev - m_new)
        p = jnp.exp(s - m_new)
        l_ref[...] = alpha * l_ref[...] + jnp.sum(p, axis=1, keepdims=True)
        p = p.astype(BF16)
        pv = None
        for idx, vb in enumerate(v_list):
            w = vb.shape[0]
            part = _dot(p[:, idx * PAGE:idx * PAGE + w] if w == PAGE else p[:, :w], vb)
            pv = part if pv is None else pv + part
        acc_ref[...] = alpha * acc_ref[...] + pv
        m_ref[...] = m_new

    wq = wq_ref[...]

    def page(ref):
        return jnp.concatenate([ref[pl.ds(h, PAGE, stride=N_HEADS), :] for h in range(N_HEADS)],
                               axis=1).astype(BF16)

    def page_scores():
        return [_dot(wq, page(k_refs[idx]), NT) for idx in range(g)]

    @pl.when(step < n_steps - 1)
    def _():
        update(jnp.concatenate(page_scores(), axis=1), [page(v_refs[idx]) for idx in range(g)])

    @pl.when(step == n_steps - 1)
    def _():
        sc = page_scores()
        sc[g - 1] = sc[g - 1] + bias_sp_ref[...]
        update(jnp.concatenate(sc, axis=1), [page(v_refs[idx]) for idx in range(g)])
        own_ref[0, 0:4, :] = kn_ref[...]
        own_ref[1, 0:4, :] = vn_ref[...]
        update(_dot(wq, own_ref[0].astype(BF16), NT) + bias_sn_ref[...], [own_ref[1].astype(BF16)])
        o = acc_ref[...] / l_ref[...]
        lam = lam_ref[layer:layer + 1, :]
        for h in range(N_HEADS):
            blk = o[h * 8:(h + 1) * 8, h * HEAD_DIM:(h + 1) * HEAD_DIM]
            od = (blk - lam * pltpu.roll(blk, 4, axis=0))[0:4]
            gate = ag_ref[:, h * HEAD_DIM:(h + 1) * HEAD_DIM].astype(F32)
            o_ref[:, h * HEAD_DIM:(h + 1) * HEAD_DIM] = _head_norm_gate(
                od, gain_ref[...], gate, out_scale).astype(o_ref.dtype)


def _attn_sample(q, k_new, v_new, cache_k, cache_v, page_table, bias_sp, bias_sn, ag, lam, gain, *, layer):
    bsz, tq, width = q.shape
    n_pages = page_table.shape[1]
    g = math.gcd(n_pages, 32)
    lam_init = 0.8 - 0.6 * math.exp(-0.3 * layer)
    per_b = lambda b, s, pt: (b, 0, 0)
    const2 = lambda b, s, pt: (0, 0)

    def page_spec(idx):
        return pl.BlockSpec((None, None, PAGE * N_HEADS, HEAD_DIM),
                            lambda b, s, pt: (layer, pt[b, s * g + idx], 0, 0))

    grid_spec = pltpu.PrefetchScalarGridSpec(
        num_scalar_prefetch=1,
        grid=(bsz, n_pages // g),
        in_specs=([pl.BlockSpec((None, tq, width), per_b)] * 3
                  + [page_spec(idx) for idx in range(g)] * 2
                  + [pl.BlockSpec(bias_sp.shape, const2), pl.BlockSpec(bias_sn.shape, const2),
                     pl.BlockSpec((None, tq, width), per_b),
                     pl.BlockSpec((SUBLANES, LANES), const2), pl.BlockSpec((1, HEAD_DIM), const2)]),
        out_specs=pl.BlockSpec((None, tq, width), per_b),
        scratch_shapes=[pltpu.VMEM((N_HEADS * 8, width), BF16), pltpu.VMEM((N_HEADS * 8, 1), F32),
                        pltpu.VMEM((N_HEADS * 8, 1), F32), pltpu.VMEM((N_HEADS * 8, width), F32),
                        pltpu.VMEM((2, PAGE, width), F32)],
    )
    return pl.pallas_call(
        functools.partial(_attn_sample_kernel, g=g, layer=layer, out_scale=1.0 - lam_init),
        out_shape=jax.ShapeDtypeStruct((bsz, tq, width), F32),
        grid_spec=grid_spec,
        compiler_params=_cparams(("parallel", "arbitrary")),
        name="attn_sample",
    )(page_table, q, k_new, v_new, *([cache_k] * g), *([cache_v] * g), bias_sp, bias_sn, ag, lam, gain)


def _unit_lower_inverses(lmats, c):
    r = lax.broadcasted_iota(jnp.int32, (c, c), 0)
    q = lax.broadcasted_iota(jnp.int32, (c, c), 1)
    eye = (r == q).astype(F32)
    base = min(c, 16)
    same_base = r // base == q // base
    zero = jnp.zeros((c, c), BF16)

    def masked(parts, mask, sign):
        return tuple(jnp.where(mask, sign * part, zero) for part in parts)

    l_parts = [_split(lm) for lm in lmats]
    p_parts = [masked(lp, same_base, -1) for lp in l_parts]
    xs = [eye + jnp.where(same_base, -lm, 0.0) for lm in lmats]
    for _ in range(int(math.log2(base)) - 1):
        p_parts = [_split(_dot3s(pp, pp)) for pp in p_parts]
        xs = [x + _dot3s(_split(x), pp) for x, pp in zip(xs, p_parts)]
    x_parts = [_split(x) for x in xs]
    s = base
    while s < c:
        below = (r // (2 * s) == q // (2 * s)) & (r // s != q // s)
        t_parts = [_split(_dot3s(masked(lp, below, 1), xp)) for lp, xp in zip(l_parts, x_parts)]
        xs = [x - _dot3s(xp, tp) for x, xp, tp in zip(xs, x_parts, t_parts)]
        x_parts = [_split(x) for x in xs]
        s *= 2
    return x_parts


def _delta_kernel(pre_ref, bd_ref, dg_ref, buf_ref, cw_ref, hp_ref, s0_ref, gain_ref,
                  o_ref, s_out_ref, conv_out_ref, s_ref, xcat_ref, *, nb, c, mix, t_valid):
    n = pl.program_id(1)
    n_chunks = pl.num_programs(1)
    units = [(b, h) for b in range(nb) for h in range(N_HEADS)]

    @pl.when(n == 0)
    def _():
        s_ref[...] = s0_ref[...]
        xcat_ref[:, 0:8, :] = buf_ref[...]

    r = lax.broadcasted_iota(jnp.int32, (c, c), 0)
    q_i = lax.broadcasted_iota(jnp.int32, (c, c), 1)
    incl = r >= q_i
    strict = r > q_i
    incl_f = incl.astype(F32)
    sel = (lax.broadcasted_iota(jnp.int32, (8, LANES), 1)
           == lax.broadcasted_iota(jnp.int32, (8, LANES), 0) + 4).astype(F32)
    eye_h = (lax.broadcasted_iota(jnp.int32, (HEAD_DIM, HEAD_DIM), 0)
             == lax.broadcasted_iota(jnp.int32, (HEAD_DIM, HEAD_DIM), 1)).astype(BF16)
    last_valid = t_valid - (t_valid - 1) // c * c

    ys, beta_ts, gcum_ts, gcum_rows = [], [], [], []
    for b in range(nb):
        xcat_ref[b, 8:8 + c, :] = pre_ref[b]
        y = xcat_ref[b, 5:5 + c, :] * cw_ref[0:1, :]
        for j in range(1, CONV_W):
            y = y + xcat_ref[b, 5 + j:5 + j + c, :] * cw_ref[j:j + 1, :]
        ys.append(_silu(y))
        conv_out_ref[b] = xcat_ref[b, last_valid:last_valid + 8, :]
        xcat_ref[b, 0:8, :] = xcat_ref[b, c:c + 8, :]

        bd = bd_ref[b]
        beta_t = _sigmoid(bd)
        g_t = -jnp.exp(hp_ref[0:1, :]) * _softplus(bd + hp_ref[1:2, :])
        if t_valid % c:
            row_ok = (lax.broadcasted_iota(jnp.int32, bd.shape, 0) + n * c) < t_valid
            beta_t = jnp.where(row_ok, beta_t, 0.0)
            g_t = jnp.where(row_ok, g_t, 0.0)
        beta_ts.append(beta_t)
        gcum_ts.append(_dot2(incl_f, g_t))
    for b in range(nb):
        gcum_rows.append(_dot2(sel, gcum_ts[b], NT))

    qn, kn, vb_, kb, e_g, decay, g_last, gcum = {}, {}, {}, {}, {}, {}, {}, {}
    for u in units:
        b, h = u
        lo, hi = h * HEAD_DIM, (h + 1) * HEAD_DIM
        qh = ys[b][:, lo:hi]
        kh = ys[b][:, mix + lo:mix + hi]
        qn[u] = qh * lax.rsqrt(jnp.sum(qh * qh, axis=-1, keepdims=True) + NORM_EPS) * (HEAD_DIM ** -0.5)
        kn[u] = kh * lax.rsqrt(jnp.sum(kh * kh, axis=-1, keepdims=True) + NORM_EPS)
        beta = beta_ts[b][:, h:h + 1]
        gcum[u] = gcum_ts[b][:, 4 + h:5 + h]
        g_last[u] = gcum_ts[b][c - 1:c, 4 + h:5 + h]
        diff = gcum[u] - gcum_rows[b][h:h + 1, :]
        decay[u] = jnp.where(incl, jnp.exp(jnp.where(incl, diff, 0.0)), 0.0)
        kb[u] = kn[u] * beta
        vb_[u] = ys[b][:, 2 * mix + lo:2 * mix + hi] * beta
        e_g[u] = jnp.exp(gcum[u])
    knb = {u: kn[u].astype(BF16) for u in units}
    lmat = {u: jnp.where(strict, _dot(kb[u].astype(BF16), knb[u], NT) * decay[u], 0.0) for u in units}
    a_qk = {u: jnp.where(incl, _dot(qn[u].astype(BF16), knb[u], NT) * decay[u], 0.0).astype(BF16)
            for u in units}

    tinv = dict(zip(units, _unit_lower_inverses([lmat[u] for u in units], c)))
    uw = {u: _dot3s(tinv[u], _split(jnp.concatenate([vb_[u], kb[u] * e_g[u]], axis=1))) for u in units}

    s_old = {u: s_ref[u[0], u[1]] for u in units}
    s_b = {u: s_old[u].astype(BF16) for u in units}
    ws = {u: _dot(uw[u][:, HEAD_DIM:].astype(BF16), s_b[u]) for u in units}
    qs = {u: _dot((qn[u] * e_g[u]).astype(BF16), s_b[u]) for u in units}
    k_tail_t = {u: _dot(eye_h, (kn[u] * jnp.exp(g_last[u] - gcum[u])).astype(BF16), NT).astype(BF16)
                for u in units}
    vnb = {u: (uw[u][:, :HEAD_DIM] - ws[u]).astype(BF16) for u in units}
    o = {u: qs[u] + _dot(a_qk[u], vnb[u]) for u in units}
    for u in units:
        s_ref[u[0], u[1]] = s_old[u] * jnp.exp(g_last[u]) + _dot(k_tail_t[u], vnb[u])
    for u in units:
        b, h = u
        lo, hi = h * HEAD_DIM, (h + 1) * HEAD_DIM
        ms = jnp.mean(o[u] * o[u], axis=-1, keepdims=True)
        o_n = o[u] * lax.rsqrt(ms + NORM_EPS) * gain_ref[...]
        o_ref[b, :, lo:hi] = (o_n * dg_ref[b, :, lo:hi].astype(F32)).astype(o_ref.dtype)

    @pl.when(n == n_chunks - 1)
    def _():
        s_out_ref[...] = s_ref[...]


def _delta(pre, bd, dg, conv_buf, conv_w, hp, s0, gain, *, nb, c, t_valid):
    batch, t_pad, width3 = pre.shape
    mix = width3 // 3
    n_chunks = t_pad // c
    tok = lambda b, n: (b, n, 0)
    per_b3 = lambda b, n: (b, 0, 0)
    per_b4 = lambda b, n: (b, 0, 0, 0)
    const2 = lambda b, n: (0, 0)
    o_d, s_new, conv_out = pl.pallas_call(
        functools.partial(_delta_kernel, nb=nb, c=c, mix=mix, t_valid=t_valid),
        out_shape=(jax.ShapeDtypeStruct((batch, t_pad, mix), dg.dtype),
                   jax.ShapeDtypeStruct(s0.shape, F32),
                   jax.ShapeDtypeStruct((batch, 8, width3), F32)),
        grid=(batch // nb, n_chunks),
        in_specs=[pl.BlockSpec((nb, c, width3), tok), pl.BlockSpec((nb, c, LANES), tok),
                  pl.BlockSpec((nb, c, mix), tok),
                  pl.BlockSpec((nb, 8, width3), per_b3),
                  pl.BlockSpec((8, width3), const2),
                  pl.BlockSpec((8, LANES), const2),
                  pl.BlockSpec((nb, N_HEADS, HEAD_DIM, HEAD_DIM), per_b4),
                  pl.BlockSpec((1, HEAD_DIM), const2)],
        out_specs=(pl.BlockSpec((nb, c, mix), tok),
                   pl.BlockSpec((nb, N_HEADS, HEAD_DIM, HEAD_DIM), per_b4),
                   pl.BlockSpec((nb, 8, width3), per_b3)),
        scratch_shapes=[pltpu.VMEM((nb, N_HEADS, HEAD_DIM, HEAD_DIM), F32),
                        pltpu.VMEM((nb, c + 8, width3), F32)],
        compiler_params=_cparams(("parallel", "arbitrary")),
        name="delta",
    )(pre, bd, dg, conv_buf, conv_w, hp, s0, gain)
    return o_d, s_new, conv_out[:, 5:8, :]


def _post_kernel(x_ref, oa_ref, od_ref, wmg_ref, wbr_ref, wo_ref, lng_ref, lnb_ref, y_ref, *, d, alpha):
    x = x_ref[...]
    gates = _sigmoid(_dot(x.astype(BF16), wmg_ref[...]))
    h_a = _dot(oa_ref[...].astype(BF16), wbr_ref[0])
    h_d = _dot(od_ref[...].astype(BF16), wbr_ref[1])
    merged = gates[:, :d] * h_a + gates[:, d:] * h_d
    y = _dot(merged.astype(BF16), wo_ref[...])
    z = alpha * x + y
    mu = jnp.mean(z, axis=-1, keepdims=True)
    zc = z - mu
    var = jnp.mean(zc * zc, axis=-1, keepdims=True)
    y_ref[...] = zc * lax.rsqrt(var + LN_EPS) * lng_ref[...] + lnb_ref[...]


def _post(x, o_a, o_d, w_mg, w_br, w_o, ln_g, ln_b, *, alpha):
    m, d = x.shape
    tm = min(512, m)
    row = lambda i: (i, 0)
    c2 = lambda i: (0, 0)
    return pl.pallas_call(
        functools.partial(_post_kernel, d=d, alpha=alpha),
        out_shape=jax.ShapeDtypeStruct((m, d), F32),
        grid=(m // tm,),
        in_specs=[pl.BlockSpec((tm, d), row), pl.BlockSpec((tm, d // 2), row), pl.BlockSpec((tm, d // 2), row),
                  pl.BlockSpec(w_mg.shape, c2), pl.BlockSpec(w_br.shape, lambda i: (0, 0, 0)),
                  pl.BlockSpec(w_o.shape, c2), pl.BlockSpec((1, d), c2), pl.BlockSpec((1, d), c2)],
        out_specs=pl.BlockSpec((tm, d), row),
        compiler_params=_cparams(("parallel",)),
        name="post",
    )(x, o_a, o_d, w_mg, w_br, w_o, ln_g, ln_b)


def kernel(x_prompt, x_sample, cache_k, cache_v, state_delta, state_conv, page_table, rel_bias_table,
           w_in, conv_w, lambda_q1, lambda_k1, lambda_q2, lambda_k2, diff_norm_g, a_log, dt_bias,
           delta_norm_g, w_branch, w_o, ln_g, ln_b):
    batch, seq, d = x_prompt.shape
    dec_b, dec_t, _ = x_sample.shape
    depth = w_in.shape[0]
    mix = d // 2
    assert mix == N_HEADS * HEAD_DIM and dec_t == 4 and cache_k.shape[2] == PAGE
    alpha = (2 * depth) ** 0.25
    t_attn = min(512, seq)
    c_prompt = min(64, seq)
    c_sample = 8
    nb_prompt = math.gcd(batch, 2)
    nb_sample = math.gcd(dec_b, 2)
    n_pool = cache_k.shape[1]

    bias_p, bias_sp, bias_sn, lam = _prep(rel_bias_table, lambda_q1, lambda_k1, lambda_q2, lambda_k2,
                                          t_attn, depth)
    cache_k2 = cache_k.reshape(depth, n_pool, PAGE * N_HEADS, HEAD_DIM)
    cache_v2 = cache_v.reshape(depth, n_pool, PAGE * N_HEADS, HEAD_DIM)

    hp = x_prompt.reshape(batch * seq, d)
    hs = x_sample.reshape(dec_b * dec_t, d)
    outs = {k: [] for k in ("ks", "vs", "dp", "ds", "cp", "cs")}
    k_all = jnp.zeros((depth, batch * seq * N_HEADS, HEAD_DIM), F32)
    v_all = jnp.zeros((depth, batch * seq * N_HEADS, HEAD_DIM), F32)
    zeros_buf = jnp.zeros((batch, 8, 3 * mix), F32)
    zeros_state = jnp.zeros((batch, N_HEADS, HEAD_DIM, HEAD_DIM), F32)

    for l in range(depth):
        w_main = w_in[l, :, :8 * mix].astype(BF16)
        w_bd = jnp.pad(w_in[l, :, 8 * mix:8 * mix + 2 * N_HEADS], ((0, 0), (0, LANES - 2 * N_HEADS))).astype(BF16)
        w_mg = w_in[l, :, 8 * mix + 2 * N_HEADS:].astype(BF16)
        w_kt = w_in[l, :, mix:2 * mix].T.astype(BF16)
        w_br = w_branch[l].astype(BF16)
        w_ol = w_o[l].astype(BF16)
        cw = jnp.pad(conv_w[l], ((0, 8 - CONV_W), (0, 0)))
        hpar = jnp.zeros((8, LANES), F32)
        hpar = hpar.at[0, N_HEADS:2 * N_HEADS].set(a_log[l]).at[1, N_HEADS:2 * N_HEADS].set(dt_bias[l])
        gain_a = diff_norm_g[l][None, :]
        gain_d = delta_norm_g[l][None, :]
        lng = ln_g[l][None, :]
        lnb = ln_b[l][None, :]

        qs, k_all, kt, v_all, vb, ag, pre, dg, bd = _inproj(hp, w_main, w_bd, w_kt, (k_all, v_all),
                                                            tm=t_attn, layer=l, depth=depth)
        o_a = _attn_prompt(qs, kt, vb, bias_p, ag, lam, gain_a, batch=batch, t=t_attn, layer=l)
        o_d, s_new, c_new = _delta(pre.reshape(batch, seq, 3 * mix), bd.reshape(batch, seq, LANES),
                                   dg.reshape(batch, seq, mix), zeros_buf, cw, hpar, zeros_state, gain_d,
                                   nb=nb_prompt, c=c_prompt, t_valid=seq)
        hp = _post(hp, o_a, o_d.reshape(batch * seq, mix), w_mg, w_br, w_ol, lng, lnb, alpha=alpha)
        outs["dp"].append(s_new)
        outs["cp"].append(c_new)

        q_s, k_s, v_s, ag_s, pre_s, dg_s, bd_s = _inproj(hs, w_main, w_bd, tm=dec_b * dec_t)
        o_a = _attn_sample(q_s.reshape(dec_b, dec_t, mix), k_s.reshape(dec_b, dec_t, mix),
                           v_s.reshape(dec_b, dec_t, mix), cache_k2, cache_v2, page_table,
                           bias_sp, bias_sn, ag_s.reshape(dec_b, dec_t, mix), lam, gain_a, layer=l)

        def pad_rows(a):
            a = a.reshape(dec_b, dec_t, a.shape[-1])
            return jnp.pad(a, ((0, 0), (0, c_sample - dec_t), (0, 0)))

        buf_s = jnp.pad(state_conv[l], ((0, 0), (8 - (CONV_W - 1), 0), (0, 0)))
        o_d, s_new, c_new = _delta(pad_rows(pre_s), pad_rows(bd_s), pad_rows(dg_s), buf_s, cw, hpar,
                                   state_delta[l], gain_d, nb=nb_sample, c=c_sample, t_valid=dec_t)
        o_d = o_d[:, :dec_t].reshape(dec_b * dec_t, mix)
        hs = _post(hs, o_a.reshape(dec_b * dec_t, mix), o_d, w_mg, w_br, w_ol, lng, lnb, alpha=alpha)
        outs["ks"].append(k_s.reshape(dec_b, dec_t, N_HEADS, HEAD_DIM))
        outs["vs"].append(v_s.reshape(dec_b, dec_t, N_HEADS, HEAD_DIM))
        outs["ds"].append(s_new)
        outs["cs"].append(c_new)

    st = {k: jnp.stack(v) for k, v in outs.items()}
    kv_shape = (depth, batch, seq, N_HEADS, HEAD_DIM)
    return (hp.reshape(batch, seq, d), hs.reshape(dec_b, dec_t, d), k_all.reshape(kv_shape), v_all.reshape(kv_shape),
            st["ks"], st["vs"], st["dp"], st["ds"], st["cp"], st["cs"])
```
